```python
import jax, jax.numpy as jnp
from jax import lax
import numpy as np

D_MODEL = 2048
BATCH = 4
SEQ = 2048
DEPTH = 4
DEC_BATCH = 8
DEC_SEQ = 8
PAST_LEN = 16384
PAGE_SIZE = 128

HEAD_DIM = 128
N_ATT_HEADS = 8
D_ATT = N_ATT_HEADS * HEAD_DIM
D_LRU = D_MODEL - D_ATT
N_LRU_BLOCKS = 8
LRU_BLOCK = D_LRU // N_LRU_BLOCKS
CONV_W = 4
LRU_C = 8.0
MOBA_BLOCK = 256
MOBA_TOPK = 3
Q_CHUNK = 16
ROPE_THETA = 10000.0
D_FF = 5632
N_EXPERTS = 8
TOP_K = 2
EPS = 1e-6
D_IN = 3 * D_ATT + 2 * D_LRU

kernel_name = 'hymba_moba_rglru_step'


def rmsnorm(x, g):
    xf = x.astype(jnp.float32)
    y = xf * lax.rsqrt(jnp.mean(xf * xf, axis=-1, keepdims=True) + EPS)
    return (y * g.astype(jnp.float32)).astype(x.dtype)


def rope(x, pos):
    half = HEAD_DIM // 2
    inv = ROPE_THETA ** (-(jnp.arange(half, dtype=jnp.float32) / half))
    ang = pos.astype(jnp.float32)[:, None] * inv[None, :]
    cos = jnp.cos(ang)[None, :, None, :]
    sin = jnp.sin(ang)[None, :, None, :]
    xf = x.astype(jnp.float32)
    x1, x2 = xf[..., :half], xf[..., half:]
    return jnp.concatenate([x1 * cos - x2 * sin, x2 * cos + x1 * sin], axis=-1).astype(x.dtype)


def gather_pages(cache_l, page_table):
    b, n_pages = page_table.shape
    pages = cache_l[page_table]
    return pages.reshape(b, n_pages * PAGE_SIZE, N_ATT_HEADS, HEAD_DIM)


def moba_blocks(k_all, v_all):
    b, t, h, d = k_all.shape
    nb = -(-t // MOBA_BLOCK)
    pad = ((0, 0), (0, nb * MOBA_BLOCK - t), (0, 0), (0, 0))
    kb = jnp.pad(k_all, pad).reshape(b, nb, MOBA_BLOCK, h, d).transpose(0, 3, 1, 2, 4)
    vb = jnp.pad(v_all, pad).reshape(b, nb, MOBA_BLOCK, h, d).transpose(0, 3, 1, 2, 4)
    kmean = jnp.mean(kb.astype(jnp.float32), axis=3)
    return kb, vb, kmean


def moba_attend(q, pos, kb, vb, kmean):
    b, nq, h, d = q.shape
    nb = kb.shape[2]
    qf = q.astype(jnp.float32)
    q_blk = pos // MOBA_BLOCK
    gate = jnp.einsum('bqhd,bhnd->bhqn', qf, kmean)
    is_past = jnp.arange(nb)[None, :] < q_blk[:, None]
    gate = jnp.where(is_past[None, None], gate, -jnp.inf)
    n_sel = min(MOBA_TOPK, nb)
    _, sel = lax.top_k(gate, n_sel)
    sel_ok = sel < q_blk[None, None, :, None]
    own = jnp.broadcast_to(q_blk[None, None, :, None], (b, h, nq, 1)).astype(sel.dtype)
    blk = jnp.concatenate([sel, own], axis=-1)
    ok = jnp.concatenate([sel_ok, jnp.ones((b, h, nq, 1), bool)], axis=-1)
    bi = jnp.arange(b)[:, None, None, None]
    hi = jnp.arange(h)[None, :, None, None]
    kg = kb[bi, hi, blk]
    vg = vb[bi, hi, blk]
    kpos = blk[..., None] * MOBA_BLOCK + jnp.arange(MOBA_BLOCK)
    mask = ok[..., None] & (kpos <= pos[None, None, :, None, None])
    s = jnp.einsum('bqhd,bhqnkd->bhqnk', qf, kg.astype(jnp.float32)) * (HEAD_DIM ** -0.5)
    s = jnp.where(mask, s, -jnp.inf).reshape(b, h, nq, -1)
    p = jax.nn.softmax(s, axis=-1).reshape(b, h, nq, n_sel + 1, MOBA_BLOCK)
    o = jnp.einsum('bhqnk,bhqnkd->bqhd', p, vg.astype(jnp.float32))
    return o.astype(q.dtype)


def causal_conv(xl, conv0, w, bias):
    s = xl.shape[1]
    xp = jnp.concatenate([conv0.astype(xl.dtype), xl], axis=1)
    y = bias
    for j in range(CONV_W):
        y = y + xp[:, j:j + s] * w[j]
    return y, xp[:, xp.shape[1] - (CONV_W - 1):]


def _lin_combine(left, right):
    a1, b1 = left
    a2, b2 = right
    return a1 * a2, a2 * b1 + b2


def rg_lru(xc, h0, w_a, b_a, w_x, b_x, lam):
    b, s, _ = xc.shape
    xb = xc.reshape(b, s, N_LRU_BLOCKS, LRU_BLOCK)
    r = jax.nn.sigmoid((jnp.einsum('bsnd,nde->bsne', xb, w_a).reshape(b, s, D_LRU) + b_a).astype(jnp.float32))
    i = jax.nn.sigmoid((jnp.einsum('bsnd,nde->bsne', xb, w_x).reshape(b, s, D_LRU) + b_x).astype(jnp.float32))
    log_a = -LRU_C * r * jax.nn.softplus(-lam.astype(jnp.float32))
    a = jnp.exp(log_a)
    u = jnp.sqrt(-jnp.expm1(2.0 * log_a)) * (i * xc.astype(jnp.float32))
    u = u.at[:, 0].add(a[:, 0] * h0.astype(jnp.float32))
    _, h = lax.associative_scan(_lin_combine, (a, u), axis=1)
    return h.astype(xc.dtype), h[:, -1].astype(xc.dtype)


def mixer_block(xn, pos, k_past, v_past, h0, conv0, w_in_l, g_q_l, g_k_l, conv_w_l, conv_b_l,
                w_a_l, b_a_l, w_x_l, b_x_l, lam_l, g_oa_l, g_ol_l, w_out_l):
    b, s, _ = xn.shape
    proj = xn @ w_in_l
    q, k, v, xl, gl = jnp.split(proj, [D_ATT, 2 * D_ATT, 3 * D_ATT, 3 * D_ATT + D_LRU], axis=-1)
    q = rope(rmsnorm(q.reshape(b, s, N_ATT_HEADS, HEAD_DIM), g_q_l), pos)
    k = rope(rmsnorm(k.reshape(b, s, N_ATT_HEADS, HEAD_DIM), g_k_l), pos)
    v = v.reshape(b, s, N_ATT_HEADS, HEAD_DIM)
    if k_past is None:
        k_all, v_all = k, v
    else:
        k_all = jnp.concatenate([k_past.astype(k.dtype), k], axis=1)
        v_all = jnp.concatenate([v_past.astype(v.dtype), v], axis=1)
    kb, vb, kmean = moba_blocks(k_all, v_all)
    if s > Q_CHUNK and s % Q_CHUNK == 0:
        nc = s // Q_CHUNK
        qc = q.reshape(b, nc, Q_CHUNK, N_ATT_HEADS, HEAD_DIM).transpose(1, 0, 2, 3, 4)
        pc = pos.reshape(nc, Q_CHUNK)
        oc = lax.map(lambda qp: moba_attend(qp[0], qp[1], kb, vb, kmean), (qc, pc))
        o = oc.transpose(1, 0, 2, 3, 4).reshape(b, s, N_ATT_HEADS, HEAD_DIM)
    else:
        o = moba_attend(q, pos, kb, vb, kmean)
    y_att = rmsnorm(o.reshape(b, s, D_ATT), g_oa_l)
    xc, conv_new = causal_conv(xl, conv0, conv_w_l, conv_b_l)
    h, h_last = rg_lru(xc, h0, w_a_l, b_a_l, w_x_l, b_x_l, lam_l)
    y_lru = rmsnorm(h * jax.nn.gelu(gl), g_ol_l)
    y = jnp.concatenate([y_att, y_lru], axis=-1) @ w_out_l
    return y, k, v, h_last, conv_new


def swiglu(x, w1, w3, w2):
    return (jax.nn.silu(x @ w1) * (x @ w3)) @ w2


def moe_swiglu(x, wr, w1, w3, w2):
    logits = (x @ wr).astype(jnp.float32)
    top_v, top_i = lax.top_k(logits, TOP_K)
    p = jax.nn.softmax(top_v, axis=-1)
    comb = jnp.sum(jax.nn.one_hot(top_i, N_EXPERTS, dtype=jnp.float32) * p[..., None], axis=-2)
    out = jnp.zeros(x.shape, jnp.float32)
    for e in range(N_EXPERTS):
        out = out + comb[..., e:e + 1] * swiglu(x, w1[e], w3[e], w2[e]).astype(jnp.float32)
    return out.astype(x.dtype)


def setup_inputs(seed: int = 0) -> dict:
    key = jax.random.key(seed)
    ks = iter(jax.random.split(key, 40))
    f32 = jnp.float32

    def nrm(shape, scale):
        return jax.random.normal(next(ks), shape, f32) * scale

    n_pages = PAST_LEN // PAGE_SIZE
    n_pool = (DEC_BATCH * n_pages * 5) // 4
    n_dense = (DEPTH + 1) // 2
    n_moe = DEPTH // 2
    x_prompt = nrm((BATCH, SEQ, D_MODEL), 1.0)
    x_sample = nrm((DEC_BATCH, DEC_SEQ, D_MODEL), 1.0)
    cache_k = nrm((DEPTH, n_pool, PAGE_SIZE, N_ATT_HEADS, HEAD_DIM), 1.0)
    cache_v = nrm((DEPTH, n_pool, PAGE_SIZE, N_ATT_HEADS, HEAD_DIM), 1.0)
    state_lru_h = nrm((DEPTH, DEC_BATCH, D_LRU), 0.5)
    state_lru_conv = nrm((DEPTH, DEC_BATCH, CONV_W - 1, D_LRU), 1.0)
    perm = jax.random.permutation(next(ks), n_pool)
    page_table = perm[:DEC_BATCH * n_pages].reshape(DEC_BATCH, n_pages).astype(jnp.int32)
    c_prompt = nrm((BATCH, D_MODEL), 1.0)
    c_sample = nrm((DEC_BATCH, D_MODEL), 1.0)
    g_norm_mix = 1.0 + nrm((DEPTH, D_MODEL), 0.05)
    g_norm_ffn = 1.0 + nrm((DEPTH, D_MODEL), 0.05)
    w_mod = nrm((DEPTH, D_MODEL, 6 * D_MODEL), 0.2 * D_MODEL ** -0.5)
    b_mod = nrm((DEPTH, 6 * D_MODEL), 0.02)
    w_in = nrm((DEPTH, D_MODEL, D_IN), D_MODEL ** -0.5)
    g_q = 1.0 + nrm((DEPTH, HEAD_DIM), 0.05)
    g_k = 1.0 + nrm((DEPTH, HEAD_DIM), 0.05)
    conv_w = nrm((DEPTH, CONV_W, D_LRU), CONV_W ** -0.5)
    conv_b = nrm((DEPTH, D_LRU), 0.02)
    w_gate_a = nrm((DEPTH, N_LRU_BLOCKS, LRU_BLOCK, LRU_BLOCK), LRU_BLOCK ** -0.5)
    b_gate_a = nrm((DEPTH, D_LRU), 0.02)
    w_gate_x = nrm((DEPTH, N_LRU_BLOCKS, LRU_BLOCK, LRU_BLOCK), LRU_BLOCK ** -0.5)
    b_gate_x = nrm((DEPTH, D_LRU), 0.02)
    u = jax.random.uniform(next(ks), (DEPTH, D_LRU), f32, minval=0.9, maxval=0.999)
    a_base = u ** (1.0 / LRU_C)
    lru_lambda = jnp.log(a_base) - jnp.log1p(-a_base)
    g_out_att = 1.0 + nrm((DEPTH, D_ATT), 0.05)
    g_out_lru = 1.0 + nrm((DEPTH, D_LRU), 0.05)
    w_out = nrm((DEPTH, D_MODEL, D_MODEL), D_MODEL ** -0.5)
    ffn_w1 = nrm((n_dense, D_MODEL, D_FF), D_MODEL ** -0.5)
    ffn_w3 = nrm((n_dense, D_MODEL, D_FF), D_MODEL ** -0.5)
    ffn_w2 = nrm((n_dense, D_FF, D_MODEL), D_FF ** -0.5)
    router_w = nrm((n_moe, D_MODEL, N_EXPERTS), D_MODEL ** -0.5)
    moe_w1 = nrm((n_moe, N_EXPERTS, D_MODEL, D_FF), D_MODEL ** -0.5)
    moe_w3 = nrm((n_moe, N_EXPERTS, D_MODEL, D_FF), D_MODEL ** -0.5)
    moe_w2 = nrm((n_moe, N_EXPERTS, D_FF, D_MODEL), D_FF ** -0.5)
    return {'x_prompt': x_prompt, 'x_sample': x_sample, 'cache_k': cache_k, 'cache_v': cache_v,
            'state_lru_h': state_lru_h, 'state_lru_conv': state_lru_conv, 'page_table': page_table,
            'c_prompt': c_prompt, 'c_sample': c_sample, 'g_norm_mix': g_norm_mix, 'g_norm_ffn': g_norm_ffn,
            'w_mod': w_mod, 'b_mod': b_mod, 'w_in': w_in, 'g_q': g_q, 'g_k': g_k,
            'conv_w': conv_w, 'conv_b': conv_b, 'w_gate_a': w_gate_a, 'b_gate_a': b_gate_a,
            'w_gate_x': w_gate_x, 'b_gate_x': b_gate_x, 'lru_lambda': lru_lambda,
            'g_out_att': g_out_att, 'g_out_lru': g_out_lru, 'w_out': w_out,
            'ffn_w1': ffn_w1, 'ffn_w3': ffn_w3, 'ffn_w2': ffn_w2, 'router_w': router_w,
            'moe_w1': moe_w1, 'moe_w3': moe_w3, 'moe_w2': moe_w2}


def reference(x_prompt, x_sample, cache_k, cache_v, state_lru_h, state_lru_conv, page_table,
              c_prompt, c_sample, g_norm_mix, g_norm_ffn, w_mod, b_mod, w_in, g_q, g_k,
              conv_w, conv_b, w_gate_a, b_gate_a, w_gate_x, b_gate_x, lru_lambda,
              g_out_att, g_out_lru, w_out, ffn_w1, ffn_w3, ffn_w2,
              router_w, moe_w1, moe_w3, moe_w2):

    def trunk(x, c, pos, h_init, conv_init, paged):
        cond = jax.nn.silu(c)
        k_rows, v_rows, h_fin, conv_fin = [], [], [], []
        for l in range(DEPTH):
            mod = (cond @ w_mod[l] + b_mod[l])[:, None, :]
            sh1, sc1, gt1, sh2, sc2, gt2 = jnp.split(mod, 6, axis=-1)
            xn = rmsnorm(x, g_norm_mix[l]) * (1 + sc1) + sh1
            if paged:
                k_past = gather_pages(cache_k[l], page_table)
                v_past = gather_pages(cache_v[l], page_table)
            else:
                k_past, v_past = None, None
            y, k_new, v_new, h_last, conv_new = mixer_block(
                xn, pos, k_past, v_past, h_init[l], conv_init[l], w_in[l], g_q[l], g_k[l],
                conv_w[l], conv_b[l], w_gate_a[l], b_gate_a[l], w_gate_x[l], b_gate_x[l],
                lru_lambda[l], g_out_att[l], g_out_lru[l], w_out[l])
            x = x + gt1 * y
            xn = rmsnorm(x, g_norm_ffn[l]) * (1 + sc2) + sh2
            if l % 2 == 0:
                f = swiglu(xn, ffn_w1[l // 2], ffn_w3[l // 2], ffn_w2[l // 2])
            else:
                f = moe_swiglu(xn, router_w[l // 2], moe_w1[l // 2], moe_w3[l // 2], moe_w2[l // 2])
            x = x + gt2 * f
            k_rows.append(k_new)
            v_rows.append(v_new)
            h_fin.append(h_last)
            conv_fin.append(conv_new)
        return x, jnp.stack(k_rows), jnp.stack(v_rows), jnp.stack(h_fin), jnp.stack(conv_fin)

    b_p, s_p, _ = x_prompt.shape
    pos_p = jnp.arange(s_p, dtype=jnp.int32)
    h0_p = jnp.zeros((DEPTH, b_p, D_LRU), x_prompt.dtype)
    conv0_p = jnp.zeros((DEPTH, b_p, CONV_W - 1, D_LRU), x_prompt.dtype)
    y_prompt, k_prompt, v_prompt, h_prompt, conv_prompt = trunk(x_prompt, c_prompt, pos_p, h0_p, conv0_p, False)

    past_len = page_table.shape[1] * PAGE_SIZE
    pos_s = past_len + jnp.arange(x_sample.shape[1], dtype=jnp.int32)
    y_sample, k_sample, v_sample, h_sample, conv_sample = trunk(x_sample, c_sample, pos_s, state_lru_h, state_lru_conv, True)

    return (y_prompt, y_sample, k_prompt, v_prompt, h_prompt, conv_prompt, k_sample, v_sample, h_sample, conv_sample)
```

```python
import functools

import jax
import jax.numpy as jnp
from jax import lax
from jax.experimental import pallas as pl
from jax.experimental.pallas import tpu as pltpu

BF = jnp.bfloat16
F32 = jnp.float32

HEAD_DIM = 128
PAGE_SIZE = 128
MOBA_BLOCK = 256
MOBA_TOPK = 3
PAGES_PER_BLOCK = MOBA_BLOCK // PAGE_SIZE
CONV_W = 4
LRU_C = 8.0
LRU_BLOCK = 128
TOP_K = 2
ROPE_THETA = 10000.0
EPS = 1e-6
LANES = 128
SUBLANES = 8
NEG_INF = float("-inf")

TM_MATMUL = 1024
TN_MATMUL = 1024
TM_NORM = 256
TM_FFN = 1024
TF_FFN = 256
T_LRU = 256
TM_COMBINE = 256
TN_MOD = 1024
PAGES_PER_STEP = 8
VMEM_LIMIT = 56 << 20


def _params(n_axes, vmem=VMEM_LIMIT):
    return pltpu.CompilerParams(dimension_semantics=("arbitrary",) * n_axes, vmem_limit_bytes=vmem)


def _tile(n, pref):
    t = min(n, pref)
    assert n % t == 0, (n, pref)
    return t


def _nt_dot(a, b):
    return lax.dot_general(a, b, (((1,), (1,)), ((), ())), preferred_element_type=F32)


def _dot(a, b):
    return jnp.dot(a, b, preferred_element_type=F32)


def _mod_body(c_ref, w_ref, b_ref, o_ref):
    c = c_ref[...]
    cond = (c * jax.nn.sigmoid(c)).astype(BF)
    o_ref[0] = _dot(cond, w_ref[0].astype(BF)) + b_ref[0]


def _modulation(c_all, w_mod, b_mod):
    depth, d, n = w_mod.shape
    rows = c_all.shape[0]
    tn = _tile(n, TN_MOD)
    return pl.pallas_call(
        _mod_body,
        grid=(depth, n // tn),
        in_specs=[pl.BlockSpec((rows, d), lambda l, j: (0, 0)),
                  pl.BlockSpec((1, d, tn), lambda l, j: (l, 0, j)),
                  pl.BlockSpec((1, 1, tn), lambda l, j: (l, 0, j))],
        out_specs=pl.BlockSpec((1, rows, tn), lambda l, j: (l, 0, j)),
        out_shape=jax.ShapeDtypeStruct((depth, rows, n), F32),
        compiler_params=_params(2),
        name="modulation",
    )(c_all, w_mod, b_mod.reshape(depth, 1, n))


def _resid_norm_body(*refs, has_res, do_norm):
    refs = list(refs)
    x_ref = refs.pop(0)
    x = x_ref[...]
    if has_res:
        f_ref, gt_ref = refs.pop(0), refs.pop(0)
        x = x + gt_ref[0] * f_ref[...]
    if do_norm:
        g_ref, sc_ref, sh_ref = refs.pop(0), refs.pop(0), refs.pop(0)
    if has_res:
        refs.pop(0)[...] = x
    if do_norm:
        ms = jnp.mean(x * x, axis=-1, keepdims=True)
        y = x * lax.rsqrt(ms + EPS) * g_ref[0]
        xn_ref = refs.pop(0)
        xn_ref[...] = (y * (1.0 + sc_ref[0]) + sh_ref[0]).astype(xn_ref.dtype)


def _resid_norm(x, *, f=None, gate_mod=None, gate_chunk=None, g_norm=None, layer=None, norm_mod=None,
                scale_chunk=None, shift_chunk=None, xn_dtype=None):
    m, d = x.shape
    has_res, do_norm = f is not None, g_norm is not None
    groups, r, _ = (gate_mod if has_res else norm_mod).shape
    rows_per_group = m // groups
    tm = _tile(rows_per_group, TM_NORM)
    assert r in (1, tm)
    per_group = rows_per_group // tm

    def mod_spec(chunk):
        return pl.BlockSpec((1, r, d), lambda i: (i // per_group, 0, chunk))

    row_spec = pl.BlockSpec((tm, d), lambda i: (i, 0))
    args, in_specs, out_specs, out_shape = [x], [row_spec], [], []
    if has_res:
        args += [f, gate_mod]
        in_specs += [row_spec, mod_spec(gate_chunk)]
        out_specs.append(row_spec)
        out_shape.append(jax.ShapeDtypeStruct((m, d), F32))
    if do_norm:
        assert norm_mod.shape[:2] == (groups, r)
        args += [g_norm, norm_mod, norm_mod]
        in_specs += [pl.BlockSpec((1, 1, d), lambda i: (layer, 0, 0)), mod_spec(scale_chunk), mod_spec(shift_chunk)]
        out_specs.append(row_spec)
        out_shape.append(jax.ShapeDtypeStruct((m, d), xn_dtype or BF))
    outs = pl.pallas_call(
        functools.partial(_resid_norm_body, has_res=has_res, do_norm=do_norm),
        grid=(m // tm,),
        in_specs=in_specs, out_specs=out_specs, out_shape=out_shape,
        compiler_params=_params(1),
        name="resid_norm",
    )(*args)
    x_new = outs[0] if has_res else x
    xn = outs[-1] if do_norm else None
    return x_new, xn


def _in_proj_body(a_ref, w_ref, o_ref, wbf_ref):
    @pl.when(pl.program_id(1) == 0)
    def _():
        wbf_ref[...] = w_ref[0].astype(BF)

    o_ref[...] = _dot(a_ref[...], wbf_ref[...])


def _in_proj(xn, w_in, layer):
    m, d = xn.shape
    n = w_in.shape[2]
    tm, tn = _tile(m, TM_MATMUL), _tile(n, TN_MATMUL)
    return pl.pallas_call(
        _in_proj_body,
        grid=(n // tn, m // tm),
        in_specs=[pl.BlockSpec((tm, d), lambda j, i: (i, 0)),
                  pl.BlockSpec((1, d, tn), lambda j, i: (layer, 0, j))],
        out_specs=pl.BlockSpec((tm, tn), lambda j, i: (i, j)),
        out_shape=jax.ShapeDtypeStruct((m, n), F32),
        scratch_shapes=[pltpu.VMEM((d, tn), BF)],
        compiler_params=_params(2),
        name="in_proj",
    )(xn, w_in)


def _head_norm_rope(x, g, cos, sin_signed):
    ms = jnp.mean(x * x, axis=-1, keepdims=True)
    y = x * lax.rsqrt(ms + EPS) * g
    return y * cos + pltpu.roll(y, HEAD_DIM // 2, axis=1) * sin_signed


def _prep_body(q_ref, k_ref, v_ref, gq_ref, gk_ref, cos_ref, sin_ref, *out_refs, n_heads, q_scale, head_major):
    cos, sin_signed = cos_ref[...], sin_ref[...]
    gq, gk = gq_ref[0], gk_ref[0]
    if head_major:
        kleaf_ref, vleaf_ref, qhm_ref, khm_ref, vhm_ref, kmean_ref = out_refs
    else:
        kleaf_ref, vleaf_ref, q_out_ref = out_refs
    v = v_ref[...]
    vleaf_ref[...] = v
    for h in range(n_heads):
        sl = slice(h * HEAD_DIM, (h + 1) * HEAD_DIM)
        qh = _head_norm_rope(q_ref[:, sl], gq, cos, sin_signed) * q_scale
        kh = _head_norm_rope(k_ref[:, sl], gk, cos, sin_signed)
        kleaf_ref[:, sl] = kh
        if head_major:
            qhm_ref[0, h] = qh.astype(BF)
            khm_ref[0, h] = kh.astype(BF)
            vhm_ref[0, h] = v[:, sl].astype(BF)
            kmean_ref[0, :, sl] = jnp.mean(kh, axis=0, keepdims=True)
        else:
            q_out_ref[:, sl] = qh


def _qkv_prep(proj, g_q, g_k, cos, sin_signed, layer, *, batch, seq, d_att, head_major):
    m = proj.shape[0]
    n_heads = d_att // HEAD_DIM
    ts = MOBA_BLOCK if head_major else seq
    assert seq % ts == 0
    per_b = seq // ts
    col = lambda c: pl.BlockSpec((ts, d_att), lambda i: (i, c))
    gain = pl.BlockSpec((1, 1, HEAD_DIM), lambda i: (layer, 0, 0))
    table = pl.BlockSpec((ts, HEAD_DIM), lambda i: (i % per_b, 0))
    leaf = pl.BlockSpec((ts, d_att), lambda i: (i, 0))
    leaf_shape = jax.ShapeDtypeStruct((m, d_att), F32)
    if head_major:
        hm = pl.BlockSpec((1, n_heads, ts, HEAD_DIM), lambda i: (i // per_b, 0, i % per_b, 0))
        hm_shape = jax.ShapeDtypeStruct((batch, n_heads, seq, HEAD_DIM), BF)
        out_specs = [leaf, leaf, hm, hm, hm, pl.BlockSpec((1, 1, d_att), lambda i: (i, 0, 0))]
        out_shape = [leaf_shape, leaf_shape, hm_shape, hm_shape, hm_shape,
                     jax.ShapeDtypeStruct((m // ts, 1, d_att), F32)]
    else:
        out_specs = [leaf, leaf, leaf]
        out_shape = [leaf_shape, leaf_shape, leaf_shape]
    return pl.pallas_call(
        functools.partial(_prep_body, n_heads=n_heads, q_scale=HEAD_DIM ** -0.5, head_major=head_major),
        grid=(m // ts,),
        in_specs=[col(0), col(1), col(2), gain, gain, table, table],
        out_specs=out_specs, out_shape=out_shape,
        compiler_params=_params(1),
        name="qkv_prep",
    )(proj, proj, proj, g_q.reshape(-1, 1, HEAD_DIM), g_k.reshape(-1, 1, HEAD_DIM), cos, sin_signed)


def _top_k_lane_mask(g, k):
    lane = lax.broadcasted_iota(jnp.int32, g.shape, 1).astype(F32)
    sel = jnp.zeros(g.shape, F32)
    picks = []
    for _ in range(k):
        m = jnp.max(g, axis=-1, keepdims=True)
        hit = jnp.logical_and(g == m, m > NEG_INF)
        idx = jnp.min(jnp.where(hit, lane, float(LANES)), axis=-1, keepdims=True)
        pick = lane == idx
        sel = jnp.where(pick, 1.0, sel)
        g = jnp.where(pick, NEG_INF, g)
        picks.append(idx)
    return sel, picks


def _moba_body(q_ref, k_ref, v_ref, km_ref, o_ref, m_ref, l_ref, acc_ref, sel_ref, *, n_blocks):
    qi = pl.program_id(2)
    q = q_ref[0, 0]
    gate = _nt_dot(q, km_ref[0, 0].astype(BF))
    lane = lax.broadcasted_iota(jnp.int32, gate.shape, 1)
    gate = jnp.where(lane < qi, gate, NEG_INF)
    sel, _ = _top_k_lane_mask(gate, MOBA_TOPK)
    sel_ref[...] = sel

    start = pl.multiple_of(qi * MOBA_BLOCK, MOBA_BLOCK)
    s = _nt_dot(q, k_ref[0, 0, pl.ds(start, MOBA_BLOCK), :])
    row = lax.broadcasted_iota(jnp.int32, s.shape, 0)
    col = lax.broadcasted_iota(jnp.int32, s.shape, 1)
    s = jnp.where(col <= row, s, NEG_INF)
    m0 = jnp.max(s, axis=-1, keepdims=True)
    p = jnp.exp(s - m0)
    m_ref[...] = m0
    l_ref[...] = jnp.sum(p, axis=-1, keepdims=True)
    acc_ref[...] = _dot(p.astype(BF), v_ref[0, 0, pl.ds(start, MOBA_BLOCK), :])

    for j in range(n_blocks - 1):
        @pl.when(j < qi)
        def _(j=j):
            rows = slice(j * MOBA_BLOCK, (j + 1) * MOBA_BLOCK)
            sj = _nt_dot(q, k_ref[0, 0, rows, :])
            sj = jnp.where(sel_ref[:, j:j + 1] > 0.5, sj, NEG_INF)
            m_old = m_ref[...]
            m_new = jnp.maximum(m_old, jnp.max(sj, axis=-1, keepdims=True))
            alpha = jnp.exp(m_old - m_new)
            pj = jnp.exp(sj - m_new)
            l_ref[...] = alpha * l_ref[...] + jnp.sum(pj, axis=-1, keepdims=True)
            acc_ref[...] = alpha * acc_ref[...] + _dot(pj.astype(BF), v_ref[0, 0, rows, :])
            m_ref[...] = m_new

    o_ref[...] = acc_ref[...] / l_ref[...]


def _moba_attention(q_hm, k_hm, v_hm, kmean_pad):
    batch, n_heads, seq, _ = q_hm.shape
    n_blocks = seq // MOBA_BLOCK
    assert n_blocks <= LANES
    blk = pl.BlockSpec((1, 1, MOBA_BLOCK, HEAD_DIM), lambda b, h, i: (b, h, i, 0))
    full = pl.BlockSpec((1, 1, seq, HEAD_DIM), lambda b, h, i: (b, h, 0, 0))
    return pl.pallas_call(
        functools.partial(_moba_body, n_blocks=n_blocks),
        grid=(batch, n_heads, n_blocks),
        in_specs=[blk, full, full, pl.BlockSpec((1, 1, LANES, HEAD_DIM), lambda b, h, i: (b, h, 0, 0))],
        out_specs=pl.BlockSpec((MOBA_BLOCK, HEAD_DIM), lambda b, h, i: (b * n_blocks + i, h)),
        out_shape=jax.ShapeDtypeStruct((batch * seq, n_heads * HEAD_DIM), F32),
        scratch_shapes=[pltpu.VMEM((MOBA_BLOCK, 1), F32), pltpu.VMEM((MOBA_BLOCK, 1), F32),
                        pltpu.VMEM((MOBA_BLOCK, HEAD_DIM), F32), pltpu.VMEM((MOBA_BLOCK, LANES), F32)],
        compiler_params=_params(3),
        name="moba_attention",
    )(q_hm, k_hm, v_hm, kmean_pad)


def _page_select_body(pt_ref, q_ref, *refs, n_heads, n_past_blocks, pages_per_step):
    page_refs, (sel_ref, km_ref) = refs[:pages_per_step], refs[pages_per_step:]
    g = pl.program_id(1)

    @pl.when(g == 0)
    def _():
        km_ref[...] = jnp.zeros(km_ref.shape, F32)

    blocks_per_step = pages_per_step // PAGES_PER_BLOCK
    for r in range(blocks_per_step):
        tot = jnp.zeros((n_heads, HEAD_DIM), F32)
        for half in range(PAGES_PER_BLOCK):
            tot = tot + jnp.sum(page_refs[r * PAGES_PER_BLOCK + half][0, 0], axis=0)
        tot = tot * (1.0 / MOBA_BLOCK)
        for h in range(n_heads):
            km_ref[h, pl.ds(g * blocks_per_step + r, 1), :] = tot[h:h + 1, :]

    @pl.when(g == pl.num_programs(1) - 1)
    def _():
        for h in range(n_heads):
            sl = slice(h * HEAD_DIM, (h + 1) * HEAD_DIM)
            gate = _nt_dot(q_ref[0, :, sl].astype(BF), km_ref[h].astype(BF))
            lane = lax.broadcasted_iota(jnp.int32, gate.shape, 1)
            gate = jnp.where(lane < n_past_blocks, gate, NEG_INF)
            _, picks = _top_k_lane_mask(gate, MOBA_TOPK)
            out = jnp.full(gate.shape, -1.0, F32)
            for t, idx in enumerate(picks):
                out = jnp.where(lane == t, jnp.where(idx < float(LANES), idx, -1.0), out)
            sel_ref[0, h] = out.astype(jnp.int32)


def _page_select(q, cache_k, page_table, layer, *, n_heads):
    batch, dec_seq, d_att = q.shape
    n_pages = page_table.shape[1]
    n_past_blocks = n_pages // PAGES_PER_BLOCK
    assert n_pages % PAGES_PER_BLOCK == 0 and n_past_blocks <= LANES
    pps = _tile(n_pages, PAGES_PER_STEP)
    assert pps % PAGES_PER_BLOCK == 0

    def page_spec(r):
        return pl.BlockSpec((1, 1, PAGE_SIZE, n_heads, HEAD_DIM),
                            lambda b, g, pt: (layer, pt[b * n_pages + g * pps + r], 0, 0, 0))

    grid_spec = pltpu.PrefetchScalarGridSpec(
        num_scalar_prefetch=1,
        grid=(batch, n_pages // pps),
        in_specs=[pl.BlockSpec((1, dec_seq, d_att), lambda b, g, pt: (b, 0, 0))] + [page_spec(r) for r in range(pps)],
        out_specs=pl.BlockSpec((1, n_heads, dec_seq, LANES), lambda b, g, pt: (b, 0, 0, 0)),
        scratch_shapes=[pltpu.VMEM((n_heads, LANES, HEAD_DIM), F32)],
    )
    return pl.pallas_call(
        functools.partial(_page_select_body, n_heads=n_heads, n_past_blocks=n_past_blocks, pages_per_step=pps),
        grid_spec=grid_spec,
        out_shape=jax.ShapeDtypeStruct((batch, n_heads, dec_seq, LANES), jnp.int32),
        compiler_params=_params(2),
        name="page_select",
    )(page_table.reshape(-1), q, *([cache_k] * pps))


def _page_attend_body(sel_ref, pt_ref, q_ref, kn_ref, vn_ref, ck_ref, cv_ref, o_ref, kbuf, vbuf, sem,
                      *, layer, n_heads, n_pages, dec_seq):
    b, h = pl.program_id(0), pl.program_id(1)
    n_pairs = dec_seq * MOBA_TOPK

    def selected(n):
        return sel_ref[((b * n_heads + h) * dec_seq + n // MOBA_TOPK) * MOBA_TOPK + n % MOBA_TOPK]

    def copies(n):
        blk = jnp.maximum(selected(n), 0)
        out = []
        for half in range(PAGES_PER_BLOCK):
            page = pt_ref[b * n_pages + blk * PAGES_PER_BLOCK + half]
            rows = pl.ds(half * PAGE_SIZE, PAGE_SIZE)
            out.append(pltpu.make_async_copy(ck_ref.at[layer, page, :, h, :], kbuf.at[n, rows, :], sem.at[0]))
            out.append(pltpu.make_async_copy(cv_ref.at[layer, page, :, h, :], vbuf.at[n, rows, :], sem.at[1]))
        return out

    for n in range(n_pairs):
        for c in copies(n):
            c.start()

    kbuf[n_pairs] = jnp.zeros((MOBA_BLOCK, HEAD_DIM), F32)
    vbuf[n_pairs] = jnp.zeros((MOBA_BLOCK, HEAD_DIM), F32)
    kbuf[n_pairs, 0:dec_seq, :] = kn_ref[0]
    vbuf[n_pairs, 0:dec_seq, :] = vn_ref[0]

    q = q_ref[0].astype(BF)
    s = _nt_dot(q, kbuf[n_pairs].astype(BF))
    row = lax.broadcasted_iota(jnp.int32, s.shape, 0)
    col = lax.broadcasted_iota(jnp.int32, s.shape, 1)
    s = jnp.where(col <= row, s, NEG_INF)
    m = jnp.max(s, axis=-1, keepdims=True)
    p = jnp.exp(s - m)
    l = jnp.sum(p, axis=-1, keepdims=True)
    acc = _dot(p.astype(BF), vbuf[n_pairs].astype(BF))

    for n in range(n_pairs):
        for c in copies(n):
            c.wait()

    for n in range(n_pairs):
        sn = _nt_dot(q, kbuf[n].astype(BF))
        mine = jnp.logical_and(row == n // MOBA_TOPK, selected(n) >= 0)
        sn = jnp.where(mine, sn, NEG_INF)
        m_new = jnp.maximum(m, jnp.max(sn, axis=-1, keepdims=True))
        alpha = jnp.exp(m - m_new)
        pn = jnp.exp(sn - m_new)
        l = alpha * l + jnp.sum(pn, axis=-1, keepdims=True)
        acc = alpha * acc + _dot(pn.astype(BF), vbuf[n].astype(BF))
        m = m_new

    o_ref[0] = acc / l


def _page_attend(sel, page_table, q, k_new, v_new, cache_k, cache_v, layer, *, n_heads):
    batch, dec_seq, d_att = q.shape
    n_pages = page_table.shape[1]
    n_slots = dec_seq * MOBA_TOPK + 1
    head = pl.BlockSpec((1, dec_seq, HEAD_DIM), lambda b, h, s_, p_: (b, 0, h))
    grid_spec = pltpu.PrefetchScalarGridSpec(
        num_scalar_prefetch=2,
        grid=(batch, n_heads),
        in_specs=[head, head, head, pl.BlockSpec(memory_space=pl.ANY), pl.BlockSpec(memory_space=pl.ANY)],
        out_specs=head,
        scratch_shapes=[pltpu.VMEM((n_slots, MOBA_BLOCK, HEAD_DIM), F32),
                        pltpu.VMEM((n_slots, MOBA_BLOCK, HEAD_DIM), F32),
                        pltpu.SemaphoreType.DMA((2,))],
    )
    return pl.pallas_call(
        functools.partial(_page_attend_body, layer=layer, n_heads=n_heads, n_pages=n_pages, dec_seq=dec_seq),
        grid_spec=grid_spec,
        out_shape=jax.ShapeDtypeStruct((batch, dec_seq, d_att), F32),
        compiler_params=_params(2),
        name="page_attend",
    )(sel, page_table.reshape(-1), q, k_new, v_new, cache_k, cache_v)


def _shift_rows(x, d, fill):
    row = lax.broadcasted_iota(jnp.int32, x.shape, 0)
    return jnp.where(row >= d, pltpu.roll(x, d, axis=0), fill)


def _lru_body(xl_ref, gl_ref, h0_ref, c0_ref, cw_ref, cb_ref, wa_ref, ba_ref, wx_ref, bx_ref, lam_ref, gol_ref,
              y_ref, hlast_ref, cnew_ref, xp_ref, h_ref, *, tc, n_blocks):
    step = pl.program_id(1)
    pad = SUBLANES
    hist = CONV_W - 1

    @pl.when(step == 0)
    def _():
        xp_ref[...] = jnp.zeros(xp_ref.shape, F32)
        xp_ref[pad - hist:pad, :] = c0_ref[0]
        h_ref[...] = h0_ref[0]

    xp_ref[pad:pad + tc, :] = xl_ref[...]
    xc = cb_ref[0]
    for j in range(CONV_W):
        xc = xc + xp_ref[pad - hist + j:pad - hist + j + tc, :] * cw_ref[0, j:j + 1, :]
    new_hist = xp_ref[pad + tc - hist:pad + tc, :]
    xp_ref[pad - hist:pad, :] = new_hist

    xcb = xc.astype(BF)
    ra, ix = [], []
    for n in range(n_blocks):
        sl = slice(n * LRU_BLOCK, (n + 1) * LRU_BLOCK)
        ra.append(_dot(xcb[:, sl], wa_ref[0, n].astype(BF)))
        ix.append(_dot(xcb[:, sl], wx_ref[0, n].astype(BF)))
    r = jax.nn.sigmoid(jnp.concatenate(ra, axis=-1) + ba_ref[0])
    i = jax.nn.sigmoid(jnp.concatenate(ix, axis=-1) + bx_ref[0])
    neg_lam = -lam_ref[0]
    softplus = jnp.maximum(neg_lam, 0.0) + jnp.log1p(jnp.exp(-jnp.abs(neg_lam)))
    log_a = (-LRU_C) * r * softplus
    a = jnp.exp(log_a)
    u = jnp.sqrt(-jnp.tanh(log_a) * (a * a + 1.0)) * (i * xc)

    d = 1
    while d < tc:
        a_prev = _shift_rows(a, d, 1.0)
        u_prev = _shift_rows(u, d, 0.0)
        u = a * u_prev + u
        a = a * a_prev
        d *= 2
    h = a * h_ref[...] + u
    h_ref[...] = h[tc - 1:tc, :]

    gl = gl_ref[...]
    gelu = 0.5 * gl * (1.0 + jnp.tanh(0.7978845608028654 * (gl + 0.044715 * (gl * gl * gl))))
    yl = h * gelu
    ms = jnp.mean(yl * yl, axis=-1, keepdims=True)
    y_ref[...] = yl * lax.rsqrt(ms + EPS) * gol_ref[0]

    @pl.when(step == pl.num_programs(1) - 1)
    def _():
        hlast_ref[0] = h[tc - 1:tc, :]
        cnew_ref[0] = new_hist


def _lru_mixer(proj, h0, conv0, conv_w, conv_b, w_a, b_a, w_x, b_x, lam, g_ol, layer, *, batch, seq, d_att, d_lru):
    assert d_att == d_lru
    m = proj.shape[0]
    tc = _tile(seq, T_LRU)
    per_b = seq // tc
    n_blocks = d_lru // LRU_BLOCK
    depth = conv_b.shape[0]
    vec = lambda a: a.reshape(depth, 1, d_lru)
    vec_spec = pl.BlockSpec((1, 1, d_lru), lambda b, s: (layer, 0, 0))
    gate_w = pl.BlockSpec((1, n_blocks, LRU_BLOCK, LRU_BLOCK), lambda b, s: (layer, 0, 0, 0))
    state = pl.BlockSpec((1, 1, d_lru), lambda b, s: (b, 0, 0))
    hist = pl.BlockSpec((1, CONV_W - 1, d_lru), lambda b, s: (b, 0, 0))
    return pl.pallas_call(
        functools.partial(_lru_body, tc=tc, n_blocks=n_blocks),
        grid=(batch, per_b),
        in_specs=[pl.BlockSpec((tc, d_lru), lambda b, s: (b * per_b + s, 3)),
                  pl.BlockSpec((tc, d_lru), lambda b, s: (b * per_b + s, 4)),
                  state, hist,
                  pl.BlockSpec((1, CONV_W, d_lru), lambda b, s: (layer, 0, 0)), vec_spec,
                  gate_w, vec_spec, gate_w, vec_spec, vec_spec, vec_spec],
        out_specs=[pl.BlockSpec((tc, d_lru), lambda b, s: (b * per_b + s, 0)), state, hist],
        out_shape=[jax.ShapeDtypeStruct((m, d_lru), F32),
                   jax.ShapeDtypeStruct((batch, 1, d_lru), F32),
                   jax.ShapeDtypeStruct((batch, CONV_W - 1, d_lru), F32)],
        scratch_shapes=[pltpu.VMEM((tc + SUBLANES, d_lru), F32), pltpu.VMEM((1, d_lru), F32)],
        compiler_params=_params(2),
        name="lru_mixer",
    )(proj, proj, h0.reshape(batch, 1, d_lru), conv0, conv_w, vec(conv_b), w_a, vec(b_a), w_x, vec(b_x),
      vec(lam), vec(g_ol))


def _out_proj_body(o_ref, yl_ref, goa_ref, w_ref, x_ref, gt_ref, xo_ref, wbf_ref, *, d_att):
    @pl.when(pl.program_id(1) == 0)
    def _():
        wbf_ref[...] = w_ref[0].astype(BF)

    o = o_ref[...]
    ms = jnp.mean(o * o, axis=-1, keepdims=True)
    ya = (o * lax.rsqrt(ms + EPS) * goa_ref[0]).astype(BF)
    y = _dot(ya, wbf_ref[0:d_att, :]) + _dot(yl_ref[...].astype(BF), wbf_ref[d_att:, :])
    xo_ref[...] = x_ref[...] + gt_ref[0] * y


def _out_proj(o_att, y_lru, g_oa, w_out, x, mod_g, layer, *, gate_chunk):
    m, d = x.shape
    d_att, d_lru = o_att.shape[1], y_lru.shape[1]
    groups, r, _ = mod_g.shape
    rows_per_group = m // groups
    tm = _tile(rows_per_group, TM_MATMUL // 2)
    assert r in (1, tm)
    per_group = rows_per_group // tm
    tn = _tile(d, TN_MATMUL)
    n_col = d // tn
    return pl.pallas_call(
        functools.partial(_out_proj_body, d_att=d_att),
        grid=(n_col, m // tm),
        in_specs=[pl.BlockSpec((tm, d_att), lambda j, i: (i, 0)),
                  pl.BlockSpec((tm, d_lru), lambda j, i: (i, 0)),
                  pl.BlockSpec((1, 1, d_att), lambda j, i: (layer, 0, 0)),
                  pl.BlockSpec((1, d, tn), lambda j, i: (layer, 0, j)),
                  pl.BlockSpec((tm, tn), lambda j, i: (i, j)),
                  pl.BlockSpec((1, r, tn), lambda j, i: (i // per_group, 0, gate_chunk * n_col + j))],
        out_specs=pl.BlockSpec((tm, tn), lambda j, i: (i, j)),
        out_shape=jax.ShapeDtypeStruct((m, d), F32),
        scratch_shapes=[pltpu.VMEM((d, tn), BF)],
        compiler_params=_params(2),
        name="out_proj",
    )(o_att, y_lru, g_oa.reshape(-1, 1, d_att), w_out, x, mod_g)


def _swiglu_partial(x, w1_ref, w3_ref, w2_ref):
    h1 = _dot(x, w1_ref.astype(BF))
    h3 = _dot(x, w3_ref.astype(BF))
    hidden = (h1 * jax.nn.sigmoid(h1) * h3).astype(BF)
    return _dot(hidden, w2_ref.astype(BF))


def _ffn_body(x_ref, w1_ref, w3_ref, w2_ref, o_ref):
    y = _swiglu_partial(x_ref[...], w1_ref[0], w3_ref[0], w2_ref[0])
    f = pl.program_id(1)

    @pl.when(f == 0)
    def _():
        o_ref[...] = y

    @pl.when(f > 0)
    def _():
        o_ref[...] += y


def _dense_ffn(xn, w1, w3, w2, idx):
    m, d = xn.shape
    d_ff = w1.shape[2]
    tm, tf = _tile(m, TM_FFN), _tile(d_ff, TF_FFN)
    return pl.pallas_call(
        _ffn_body,
        grid=(m // tm, d_ff // tf),
        in_specs=[pl.BlockSpec((tm, d), lambda i, f: (i, 0)),
                  pl.BlockSpec((1, d, tf), lambda i, f: (idx, 0, f)),
                  pl.BlockSpec((1, d, tf), lambda i, f: (idx, 0, f)),
                  pl.BlockSpec((1, tf, d), lambda i, f: (idx, f, 0))],
        out_specs=pl.BlockSpec((tm, d), lambda i, f: (i, 0)),
        out_shape=jax.ShapeDtypeStruct((m, d), F32),
        compiler_params=_params(2),
        name="dense_ffn",
    )(xn, w1, w3, w2)


def _router_body(x_ref, w_ref, idx_ref, p_ref, *, n_experts):
    logits = _dot(x_ref[...].astype(BF), w_ref[0].astype(BF))
    lane = lax.broadcasted_iota(jnp.int32, logits.shape, 1)
    logits = jnp.where(lane < n_experts, logits, NEG_INF)
    top = []
    g = logits
    lane_f = lane.astype(F32)
    for _ in range(TOP_K):
        mx = jnp.max(g, axis=-1, keepdims=True)
        idx = jnp.min(jnp.where(g == mx, lane_f, float(LANES)), axis=-1, keepdims=True)
        g = jnp.where(lane_f == idx, NEG_INF, g)
        top.append((mx, idx))
    (v0, i0), (v1, i1) = top
    e1 = jnp.exp(v1 - v0)
    p0 = 1.0 / (1.0 + e1)
    p1 = e1 / (1.0 + e1)
    idx_ref[...] = jnp.where(lane == 0, i0, jnp.where(lane == 1, i1, 0.0)).astype(jnp.int32)
    p_ref[...] = jnp.where(lane == 0, p0, jnp.where(lane == 1, p1, 0.0))


def _router(xn, router_w_pad, idx, n_experts):
    m, d = xn.shape
    tm = _tile(m, TM_NORM)
    row = pl.BlockSpec((tm, LANES), lambda i: (i, 0))
    return pl.pallas_call(
        functools.partial(_router_body, n_experts=n_experts),
        grid=(m // tm,),
        in_specs=[pl.BlockSpec((tm, d), lambda i: (i, 0)), pl.BlockSpec((1, d, LANES), lambda i: (idx, 0, 0))],
        out_specs=[row, row],
        out_shape=[jax.ShapeDtypeStruct((m, LANES), jnp.int32), jax.ShapeDtypeStruct((m, LANES), F32)],
        compiler_params=_params(1),
        name="router",
    )(xn, router_w_pad)


def _expert_body(te_ref, used_ref, tok_ref, x_hbm, w1_ref, w3_ref, w2_ref, p_ref, o_ref, xs_ref, xbf_ref, sem,
                 *, tm):
    t, f = pl.program_id(0), pl.program_id(1)
    active = t < used_ref[0]

    def row_copy(r):
        tok = tok_ref[t * tm + r]
        return pltpu.make_async_copy(x_hbm.at[pl.ds(tok, 1), :], xs_ref.at[pl.ds(r, 1), :], sem.at[0])

    @pl.when(jnp.logical_and(active, f == 0))
    def _():
        def issue(r, c):
            row_copy(r).start()
            return c

        def drain(r, c):
            row_copy(r).wait()
            return c

        lax.fori_loop(0, tm, issue, 0)
        lax.fori_loop(0, tm, drain, 0)
        xbf_ref[...] = xs_ref[...].astype(BF)

    @pl.when(active)
    def _():
        y = _swiglu_partial(xbf_ref[...], w1_ref[0, 0], w3_ref[0, 0], w2_ref[0, 0])

        @pl.when(f == 0)
        def _():
            o_ref[...] = y

        @pl.when(jnp.logical_and(f > 0, f < pl.num_programs(1) - 1))
        def _():
            o_ref[...] += y

        @pl.when(jnp.logical_and(f > 0, f == pl.num_programs(1) - 1))
        def _():
            o_ref[...] = (o_ref[...] + y) * p_ref[...]

    @pl.when(jnp.logical_and(jnp.logical_not(active), f == 0))
    def _():
        o_ref[...] = jnp.zeros(o_ref.shape, F32)


def _expert_ffn(xn, w1, w3, w2, idx, tile_expert, n_used, slot_token, slot_p, *, tm):
    m, d = xn.shape
    d_ff = w1.shape[3]
    n_tiles = tile_expert.shape[0]
    tf = _tile(d_ff, TF_FFN)
    n_f = d_ff // tf
    assert n_f > 1

    def fcol(t, f, used):
        return jnp.where(t < used[0], f, n_f - 1)

    grid_spec = pltpu.PrefetchScalarGridSpec(
        num_scalar_prefetch=3,
        grid=(n_tiles, n_f),
        in_specs=[pl.BlockSpec(memory_space=pl.ANY),
                  pl.BlockSpec((1, 1, d, tf), lambda t, f, te, used, tok: (idx, te[t], 0, fcol(t, f, used))),
                  pl.BlockSpec((1, 1, d, tf), lambda t, f, te, used, tok: (idx, te[t], 0, fcol(t, f, used))),
                  pl.BlockSpec((1, 1, tf, d), lambda t, f, te, used, tok: (idx, te[t], fcol(t, f, used), 0)),
                  pl.BlockSpec((tm, 1), lambda t, f, te, used, tok: (t, 0))],
        out_specs=pl.BlockSpec((tm, d), lambda t, f, te, used, tok: (t, 0)),
        scratch_shapes=[pltpu.VMEM((tm, d), F32), pltpu.VMEM((tm, d), BF), pltpu.SemaphoreType.DMA((1,))],
    )
    return pl.pallas_call(
        functools.partial(_expert_body, tm=tm),
        grid_spec=grid_spec,
        out_shape=jax.ShapeDtypeStruct((n_tiles * tm, d), F32),
        compiler_params=_params(2),
        name="expert_ffn",
    )(tile_expert, n_used, slot_token, xn, w1, w3, w2, slot_p)


def _combine_body(pos_ref, ys_hbm, o_ref, a_ref, b_ref, sem, *, tm):
    base = pl.program_id(0) * tm

    def copies(r):
        s0 = pos_ref[(base + r) * TOP_K]
        s1 = pos_ref[(base + r) * TOP_K + 1]
        return (pltpu.make_async_copy(ys_hbm.at[pl.ds(s0, 1), :], a_ref.at[pl.ds(r, 1), :], sem.at[0]),
                pltpu.make_async_copy(ys_hbm.at[pl.ds(s1, 1), :], b_ref.at[pl.ds(r, 1), :], sem.at[1]))

    def issue(r, c):
        for cp in copies(r):
            cp.start()
        return c

    def drain(r, c):
        for cp in copies(r):
            cp.wait()
        return c

    lax.fori_loop(0, tm, issue, 0)
    lax.fori_loop(0, tm, drain, 0)
    o_ref[...] = a_ref[...] + b_ref[...]


def _combine(ys, pos, m):
    d = ys.shape[1]
    tm = _tile(m, TM_COMBINE)
    grid_spec = pltpu.PrefetchScalarGridSpec(
        num_scalar_prefetch=1,
        grid=(m // tm,),
        in_specs=[pl.BlockSpec(memory_space=pl.ANY)],
        out_specs=pl.BlockSpec((tm, d), lambda i, pos_: (i, 0)),
        scratch_shapes=[pltpu.VMEM((tm, d), F32), pltpu.VMEM((tm, d), F32), pltpu.SemaphoreType.DMA((2,))],
    )
    return pl.pallas_call(
        functools.partial(_combine_body, tm=tm),
        grid_spec=grid_spec,
        out_shape=jax.ShapeDtypeStruct((m, d), F32),
        compiler_params=_params(1),
        name="moe_combine",
    )(pos, ys)


def _moe_ffn(xn, router_w_pad, w1, w3, w2, idx, n_experts):
    m, _ = xn.shape
    top_idx, top_p = _router(xn, router_w_pad, idx, n_experts)
    e_flat = top_idx[:, :TOP_K].reshape(-1)
    p_flat = top_p[:, :TOP_K].reshape(-1)
    n_assign = m * TOP_K
    tm = min(TM_FFN, m)
    n_tiles = -(-n_assign // tm) + n_experts
    one_hot = (e_flat[:, None] == jnp.arange(n_experts, dtype=jnp.int32)[None, :]).astype(jnp.int32)
    csum = jnp.cumsum(one_hot, axis=0)
    rank = jnp.sum(csum * one_hot, axis=1) - 1
    counts = csum[-1]
    tiles_e = (counts + tm - 1) // tm
    tile_end = jnp.cumsum(tiles_e)
    tile_start = tile_end - tiles_e
    slot = (jnp.sum(tile_start[None, :] * one_hot, axis=1) * tm + rank).astype(jnp.int32)
    slot_token = jnp.zeros((n_tiles * tm,), jnp.int32).at[slot].set(jnp.arange(n_assign, dtype=jnp.int32) // TOP_K)
    slot_p = jnp.zeros((n_tiles * tm,), F32).at[slot].set(p_flat).reshape(-1, 1)
    n_used = tile_end[-1]
    t_ids = jnp.minimum(jnp.arange(n_tiles, dtype=jnp.int32), n_used - 1)
    tile_expert = jnp.sum((t_ids[:, None] >= tile_end[None, :]).astype(jnp.int32), axis=1).astype(jnp.int32)
    ys = _expert_ffn(xn, w1, w3, w2, idx, tile_expert, n_used.reshape(1).astype(jnp.int32), slot_token, slot_p, tm=tm)
    return _combine(ys, slot, m)


def _rope_tables(pos):
    half = HEAD_DIM // 2
    inv = ROPE_THETA ** (-(jnp.arange(half, dtype=F32) / half))
    ang = pos.astype(F32)[:, None] * inv[None, :]
    cos, sin = jnp.cos(ang), jnp.sin(ang)
    return jnp.concatenate([cos, cos], axis=-1), jnp.concatenate([-sin, sin], axis=-1)


def _trunk(x3, mod_rows, pos, h_init, conv_init, paged, w):
    batch, seq, d = x3.shape
    m = batch * seq
    depth = w["w_in"].shape[0]
    d_att = w["g_out_att"].shape[1]
    d_lru = w["g_out_lru"].shape[1]
    n_heads = d_att // HEAD_DIM
    n_experts = w["router_w"].shape[2]
    cos, sin_signed = _rope_tables(pos)
    router_w_pad = jnp.pad(w["router_w"], ((0, 0), (0, 0), (0, LANES - n_experts)))

    x = x3.reshape(m, d)
    if seq % TM_NORM == 0:
        mods = [mod_rows[l][:, None, :] for l in range(depth)]
    else:
        mods = [jnp.repeat(mod_rows[l], seq, axis=0)[None] for l in range(depth)]

    k_rows, v_rows, h_fin, conv_fin = [], [], [], []
    f, xn = None, None
    for l in range(depth):
        if l == 0:
            _, xn = _resid_norm(x, g_norm=w["g_norm_mix"], layer=l, norm_mod=mods[l], scale_chunk=1, shift_chunk=0)
        else:
            x, xn = _resid_norm(x, f=f, gate_mod=mods[l - 1], gate_chunk=5, g_norm=w["g_norm_mix"], layer=l,
                                norm_mod=mods[l], scale_chunk=1, shift_chunk=0)
        proj = _in_proj(xn, w["w_in"], l)
        if paged is None:
            k_new, v_new, q_hm, k_hm, v_hm, kmean = _qkv_prep(
                proj, w["g_q"], w["g_k"], cos, sin_signed, l, batch=batch, seq=seq, d_att=d_att, head_major=True)
            n_blocks = seq // MOBA_BLOCK
            km = kmean.reshape(batch, n_blocks, n_heads, HEAD_DIM).transpose(0, 2, 1, 3)
            km = jnp.pad(km, ((0, 0), (0, 0), (0, LANES - n_blocks), (0, 0)))
            o_att = _moba_attention(q_hm, k_hm, v_hm, km)
        else:
            cache_k, cache_v, page_table = paged
            k_new, v_new, q = _qkv_prep(
                proj, w["g_q"], w["g_k"], cos, sin_signed, l, batch=batch, seq=seq, d_att=d_att, head_major=False)
            q3 = q.reshape(batch, seq, d_att)
            sel = _page_select(q3, cache_k, page_table, l, n_heads=n_heads)
            sel_flat = sel[..., :MOBA_TOPK].reshape(-1)
            o_att = _page_attend(sel_flat, page_table, q3, k_new.reshape(batch, seq, d_att),
                                 v_new.reshape(batch, seq, d_att), cache_k, cache_v, l,
                                 n_heads=n_heads).reshape(m, d_att)
        y_lru, h_last, conv_new = _lru_mixer(
            proj, h_init[l], conv_init[l], w["conv_w"], w["conv_b"], w["w_gate_a"], w["b_gate_a"],
            w["w_gate_x"], w["b_gate_x"], w["lru_lambda"], w["g_out_lru"], l,
            batch=batch, seq=seq, d_att=d_att, d_lru=d_lru)
        x = _out_proj(o_att, y_lru, w["g_out_att"], w["w_out"], x, mods[l], l, gate_chunk=2)
        if l % 2 == 0:
            _, xn2 = _resid_norm(x, g_norm=w["g_norm_ffn"], layer=l, norm_mod=mods[l], scale_chunk=4, shift_chunk=3)
            f = _dense_ffn(xn2, w["ffn_w1"], w["ffn_w3"], w["ffn_w2"], l // 2)
        else:
            _, xn2 = _resid_norm(x, g_norm=w["g_norm_ffn"], layer=l, norm_mod=mods[l], scale_chunk=4,
                                 shift_chunk=3, xn_dtype=F32)
            f = _moe_ffn(xn2, router_w_pad, w["moe_w1"], w["moe_w3"], w["moe_w2"], l // 2, n_experts)
        k_rows.append(k_new)
        v_rows.append(v_new)
        h_fin.append(h_last)
        conv_fin.append(conv_new)
    x, _ = _resid_norm(x, f=f, gate_mod=mods[depth - 1], gate_chunk=5)
    shape_kv = (depth, batch, seq, n_heads, HEAD_DIM)
    return (x.reshape(batch, seq, d), jnp.stack(k_rows).reshape(shape_kv), jnp.stack(v_rows).reshape(shape_kv),
            jnp.stack(h_fin).reshape(depth, batch, d_lru), jnp.stack(conv_fin))


def kernel(x_prompt, x_sample, cache_k, cache_v, state_lru_h, state_lru_conv, page_table, c_prompt, c_sample,
           g_norm_mix, g_norm_ffn, w_mod, b_mod, w_in, g_q, g_k, conv_w, conv_b, w_gate_a, b_gate_a, w_gate_x,
           b_gate_x, lru_lambda, g_out_att, g_out_lru, w_out, ffn_w1, ffn_w3, ffn_w2, router_w, moe_w1, moe_w3,
           moe_w2):
    depth, d = g_norm_mix.shape
    w = dict(g_norm_mix=g_norm_mix.reshape(depth, 1, d), g_norm_ffn=g_norm_ffn.reshape(depth, 1, d), w_in=w_in,
             g_q=g_q, g_k=g_k, conv_w=conv_w, conv_b=conv_b, w_gate_a=w_gate_a, b_gate_a=b_gate_a,
             w_gate_x=w_gate_x, b_gate_x=b_gate_x, lru_lambda=lru_lambda, g_out_att=g_out_att,
             g_out_lru=g_out_lru, w_out=w_out, ffn_w1=ffn_w1, ffn_w3=ffn_w3, ffn_w2=ffn_w2, router_w=router_w,
             moe_w1=moe_w1, moe_w3=moe_w3, moe_w2=moe_w2)
    b_p, s_p, _ = x_prompt.shape
    b_s, s_s, _ = x_sample.shape
    d_att = g_out_att.shape[1]
    d_lru = g_out_lru.shape[1]

    n_cond = b_p + b_s
    pad_rows = (-n_cond) % (2 * SUBLANES)
    c_all = jnp.concatenate([c_prompt, c_sample, jnp.zeros((pad_rows, d), F32)], axis=0)
    mod_all = _modulation(c_all, w_mod, b_mod)

    past_len = page_table.shape[1] * PAGE_SIZE
    pos_s = past_len + jnp.arange(s_s, dtype=jnp.int32)
    paged = (cache_k, cache_v, page_table)
    y_s, k_s, v_s, h_s, conv_s = _trunk(x_sample, mod_all[:, b_p:n_cond], pos_s, state_lru_h, state_lru_conv,
                                        paged, w)

    pos_p = jnp.arange(s_p, dtype=jnp.int32)
    h0_p = jnp.zeros((depth, b_p, d_lru), F32)
    conv0_p = jnp.zeros((depth, b_p, CONV_W - 1, d_lru), F32)
    y_p, k_p, v_p, h_p, conv_p = _trunk(x_prompt, mod_all[:, :b_p], pos_p, h0_p, conv0_p, None, w)
    return (y_p, y_s, k_p, v_p, h_p, conv_p, k_s, v_s, h_s, conv_s)
```

```python
import functools

import jax
import jax.numpy as jnp
from jax import lax
from jax.experimental import pallas as pl
from jax.experimental.pallas import tpu as pltpu

BF = jnp.bfloat16
F32 = jnp.float32

HEAD_DIM = 128
PAGE_SIZE = 128
MOBA_BLOCK = 256
MOBA_TOPK = 3
PAGES_PER_BLOCK = MOBA_BLOCK // PAGE_SIZE
CONV_W = 4
LRU_C = 8.0
LRU_BLOCK = 128
TOP_K = 2
ROPE_THETA = 10000.0
EPS = 1e-6
LANES = 128
SUBLANES = 8
NEG_INF = float("-inf")

TM_MATMUL = 1024
TN_MATMUL = 1024
TM_NORM = 256
TM_FFN = 1024
TF_FFN = 256
T_LRU = 256
TM_COMBINE = 256
TN_MOD = 1024
PAGES_PER_STEP = 8
VMEM_LIMIT = 56 << 20


def _params(n_axes, vmem=VMEM_LIMIT):
    return pltpu.CompilerParams(dimension_semantics=("arbitrary",) * n_axes, vmem_limit_bytes=vmem)


def _tile(n, pref):
    t = min(n, pref)
    assert n % t == 0, (n, pref)
    return t


def _nt_dot(a, b):
    return lax.dot_general(a, b, (((1,), (1,)), ((), ())), preferred_element_type=F32)


def _dot(a, b):
    return jnp.dot(a, b, preferred_element_type=F32)


def _mod_body(c_ref, w_ref, b_ref, o_ref):
    c = c_ref[...]
    cond = (c * jax.nn.sigmoid(c)).astype(BF)
    o_ref[0] = _dot(cond, w_ref[0].astype(BF)) + b_ref[0]


def _modulation(c_all, w_mod, b_mod):
    depth, d, n = w_mod.shape
    rows = c_all.shape[0]
    tn = _tile(n, TN_MOD)
    return pl.pallas_call(
        _mod_body,
        grid=(depth, n // tn),
        in_specs=[pl.BlockSpec((rows, d), lambda l, j: (0, 0)),
                  pl.BlockSpec((1, d, tn), lambda l, j: (l, 0, j)),
                  pl.BlockSpec((1, 1, tn), lambda l, j: (l, 0, j))],
        out_specs=pl.BlockSpec((1, rows, tn), lambda l, j: (l, 0, j)),
        out_shape=jax.ShapeDtypeStruct((depth, rows, n), F32),
        compiler_params=_params(2),
        name="modulation",
    )(c_all, w_mod, b_mod.reshape(depth, 1, n))


def _resid_norm_body(*refs, has_res, do_norm):
    refs = list(refs)
    x_ref = refs.pop(0)
    x = x_ref[...]
    if has_res:
        f_ref, gt_ref = refs.pop(0), refs.pop(0)
        x = x + gt_ref[0] * f_ref[...]
    if do_norm:
        g_ref, sc_ref, sh_ref = refs.pop(0), refs.pop(0), refs.pop(0)
    if has_res:
        refs.pop(0)[...] = x
    if do_norm:
        ms = jnp.mean(x * x, axis=-1, keepdims=True)
        y = x * lax.rsqrt(ms + EPS) * g_ref[0]
        xn_ref = refs.pop(0)
        xn_ref[...] = (y * (1.0 + sc_ref[0]) + sh_ref[0]).astype(xn_ref.dtype)


def _resid_norm(x, *, f=None, gate_mod=None, gate_chunk=None, g_norm=None, layer=None, norm_mod=None,
                scale_chunk=None, shift_chunk=None, xn_dtype=None):
    m, d = x.shape
    has_res, do_norm = f is not None, g_norm is not None
    groups, r, _ = (gate_mod if has_res else norm_mod).shape
    rows_per_group = m // groups
    tm = _tile(rows_per_group, TM_NORM)
    assert r in (1, tm)
    per_group = rows_per_group // tm

    def mod_spec(chunk):
        return pl.BlockSpec((1, r, d), lambda i: (i // per_group, 0, chunk))

    row_spec = pl.BlockSpec((tm, d), lambda i: (i, 0))
    args, in_specs, out_specs, out_shape = [x], [row_spec], [], []
    if has_res:
        args += [f, gate_mod]
        in_specs += [row_spec, mod_spec(gate_chunk)]
        out_specs.append(row_spec)
        out_shape.append(jax.ShapeDtypeStruct((m, d), F32))
    if do_norm:
        assert norm_mod.shape[:2] == (groups, r)
        args += [g_norm, norm_mod, norm_mod]
        in_specs += [pl.BlockSpec((1, 1, d), lambda i: (layer, 0, 0)), mod_spec(scale_chunk), mod_spec(shift_chunk)]
        out_specs.append(row_spec)
        out_shape.append(jax.ShapeDtypeStruct((m, d), xn_dtype or BF))
    outs = pl.pallas_call(
        functools.partial(_resid_norm_body, has_res=has_res, do_norm=do_norm),
        grid=(m // tm,),
        in_specs=in_specs, out_specs=out_specs, out_shape=out_shape,
        compiler_params=_params(1),
        name="resid_norm",
    )(*args)
    x_new = outs[0] if has_res else x
    xn = outs[-1] if do_norm else None
    return x_new, xn


def _in_proj_body(a_ref, w_ref, o_ref, wbf_ref):
    @pl.when(pl.program_id(1) == 0)
    def _():
        wbf_ref[...] = w_ref[0].astype(BF)

    o_ref[...] = _dot(a_ref[...], wbf_ref[...])


def _in_proj(xn, w_in, layer):
    m, d = xn.shape
    n = w_in.shape[2]
    tm, tn = _tile(m, TM_MATMUL), _tile(n, TN_MATMUL)
    return pl.pallas_call(
        _in_proj_body,
        grid=(n // tn, m // tm),
        in_specs=[pl.BlockSpec((tm, d), lambda j, i: (i, 0)),
                  pl.BlockSpec((1, d, tn), lambda j, i: (layer, 0, j))],
        out_specs=pl.BlockSpec((tm, tn), lambda j, i: (i, j)),
        out_shape=jax.ShapeDtypeStruct((m, n), F32),
        scratch_shapes=[pltpu.VMEM((d, tn), BF)],
        compiler_params=_params(2),
        name="in_proj",
    )(xn, w_in)


def _head_norm_rope(x, g, cos, sin_signed):
    ms = jnp.mean(x * x, axis=-1, keepdims=True)
    y = x * lax.rsqrt(ms + EPS) * g
    return y * cos + pltpu.roll(y, HEAD_DIM // 2, axis=1) * sin_signed


def _prep_body(q_ref, k_ref, v_ref, gq_ref, gk_ref, cos_ref, sin_ref, *out_refs, n_heads, q_scale, head_major):
    cos, sin_signed = cos_ref[...], sin_ref[...]
    gq, gk = gq_ref[0], gk_ref[0]
    if head_major:
        kleaf_ref, vleaf_ref, qt_ref, khm_ref, vt_ref, kmean_ref = out_refs
    else:
        kleaf_ref, vleaf_ref, q_out_ref = out_refs
    v = v_ref[...]
    vleaf_ref[...] = v
    for h in range(n_heads):
        sl = slice(h * HEAD_DIM, (h + 1) * HEAD_DIM)
        qh = _head_norm_rope(q_ref[:, sl], gq, cos, sin_signed) * q_scale
        kh = _head_norm_rope(k_ref[:, sl], gk, cos, sin_signed)
        kleaf_ref[:, sl] = kh
        if head_major:
            qt_ref[0, h] = qh.T.astype(BF)
            khm_ref[0, h] = kh.astype(BF)
            vt_ref[0, h] = v[:, sl].T.astype(BF)
            kmean_ref[0, :, sl] = jnp.mean(kh, axis=0, keepdims=True)
        else:
            q_out_ref[:, sl] = qh


def _qkv_prep(proj, g_q, g_k, cos, sin_signed, layer, *, batch, seq, d_att, head_major):
    m = proj.shape[0]
    n_heads = d_att // HEAD_DIM
    ts = MOBA_BLOCK if head_major else seq
    assert seq % ts == 0
    per_b = seq // ts
    col = lambda c: pl.BlockSpec((ts, d_att), lambda i: (i, c))
    gain = pl.BlockSpec((1, 1, HEAD_DIM), lambda i: (layer, 0, 0))
    table = pl.BlockSpec((ts, HEAD_DIM), lambda i: (i % per_b, 0))
    leaf = pl.BlockSpec((ts, d_att), lambda i: (i, 0))
    leaf_shape = jax.ShapeDtypeStruct((m, d_att), F32)
    if head_major:
        hm = pl.BlockSpec((1, n_heads, ts, HEAD_DIM), lambda i: (i // per_b, 0, i % per_b, 0))
        hm_shape = jax.ShapeDtypeStruct((batch, n_heads, seq, HEAD_DIM), BF)
        hmt = pl.BlockSpec((1, n_heads, HEAD_DIM, ts), lambda i: (i // per_b, 0, 0, i % per_b))
        hmt_shape = jax.ShapeDtypeStruct((batch, n_heads, HEAD_DIM, seq), BF)
        out_specs = [leaf, leaf, hmt, hm, hmt, pl.BlockSpec((1, 1, d_att), lambda i: (i, 0, 0))]
        out_shape = [leaf_shape, leaf_shape, hmt_shape, hm_shape, hmt_shape,
                     jax.ShapeDtypeStruct((m // ts, 1, d_att), F32)]
    else:
        out_specs = [leaf, leaf, leaf]
        out_shape = [leaf_shape, leaf_shape, leaf_shape]
    return pl.pallas_call(
        functools.partial(_prep_body, n_heads=n_heads, q_scale=HEAD_DIM ** -0.5, head_major=head_major),
        grid=(m // ts,),
        in_specs=[col(0), col(1), col(2), gain, gain, table, table],
        out_specs=out_specs, out_shape=out_shape,
        compiler_params=_params(1),
        name="qkv_prep",
    )(proj, proj, proj, g_q.reshape(-1, 1, HEAD_DIM), g_k.reshape(-1, 1, HEAD_DIM), cos, sin_signed)


def _top_k_lane_mask(g, k):
    lane = lax.broadcasted_iota(jnp.int32, g.shape, 1).astype(F32)
    sel = jnp.zeros(g.shape, F32)
    picks = []
    for _ in range(k):
        m = jnp.max(g, axis=-1, keepdims=True)
        hit = jnp.logical_and(g == m, m > NEG_INF)
        idx = jnp.min(jnp.where(hit, lane, float(LANES)), axis=-1, keepdims=True)
        pick = lane == idx
        sel = jnp.where(pick, 1.0, sel)
        g = jnp.where(pick, NEG_INF, g)
        picks.append(idx)
    return sel, picks


def _top_k_row_mask(g, k):
    blk = lax.broadcasted_iota(jnp.int32, g.shape, 0).astype(F32)
    sel = jnp.zeros(g.shape, F32)
    for _ in range(k):
        m = jnp.max(g, axis=0, keepdims=True)
        hit = jnp.logical_and(g == m, m > NEG_INF)
        idx = jnp.min(jnp.where(hit, blk, float(g.shape[0])), axis=0, keepdims=True)
        pick = blk == idx
        sel = jnp.where(pick, 1.0, sel)
        g = jnp.where(pick, NEG_INF, g)
    return sel


def _moba_body(qt_ref, k_ref, vt_ref, kd_ref, vtd_ref, km_ref, o_ref, m_ref, l_ref, acc_ref, sel_ref, *, n_blocks):
    qi = pl.program_id(2)
    qt = qt_ref[0, 0]
    gate = _dot(km_ref[0, 0].astype(BF), qt)
    blk = lax.broadcasted_iota(jnp.int32, gate.shape, 0)
    sel_ref[...] = _top_k_row_mask(jnp.where(blk < qi, gate, NEG_INF), MOBA_TOPK)

    s = _dot(kd_ref[0, 0], qt)
    key = lax.broadcasted_iota(jnp.int32, s.shape, 0)
    qry = lax.broadcasted_iota(jnp.int32, s.shape, 1)
    s = jnp.where(key <= qry, s, NEG_INF)
    m0 = jnp.max(s, axis=0, keepdims=True)
    p = jnp.exp(s - m0)
    m_ref[...] = m0
    l_ref[...] = jnp.sum(p, axis=0, keepdims=True)
    acc_ref[...] = _dot(vtd_ref[0, 0], p.astype(BF))

    for j in range(n_blocks - 1):
        @pl.when(j < qi)
        def _(j=j):
            span = slice(j * MOBA_BLOCK, (j + 1) * MOBA_BLOCK)
            sj = _dot(k_ref[0, 0, span, :], qt)
            sj = jnp.where(sel_ref[j:j + 1, :] > 0.5, sj, NEG_INF)
            m_old = m_ref[...]
            m_new = jnp.maximum(m_old, jnp.max(sj, axis=0, keepdims=True))
            alpha = jnp.exp(m_old - m_new)
            pj = jnp.exp(sj - m_new)
            l_ref[...] = alpha * l_ref[...] + jnp.sum(pj, axis=0, keepdims=True)
            acc_ref[...] = alpha * acc_ref[...] + _dot(vt_ref[0, 0, :, span], pj.astype(BF))
            m_ref[...] = m_new

    o_ref[...] = (acc_ref[...] / l_ref[...]).T


def _moba_attention(q_t, k_hm, v_t, kmean):
    batch, n_heads, seq, _ = k_hm.shape
    n_blocks = seq // MOBA_BLOCK
    nb_pad = kmean.shape[2]
    rows = pl.BlockSpec((1, 1, MOBA_BLOCK, HEAD_DIM), lambda b, h, i: (b, h, i, 0))
    cols = pl.BlockSpec((1, 1, HEAD_DIM, MOBA_BLOCK), lambda b, h, i: (b, h, 0, i))
    all_rows = pl.BlockSpec((1, 1, seq, HEAD_DIM), lambda b, h, i: (b, h, 0, 0))
    all_cols = pl.BlockSpec((1, 1, HEAD_DIM, seq), lambda b, h, i: (b, h, 0, 0))
    return pl.pallas_call(
        functools.partial(_moba_body, n_blocks=n_blocks),
        grid=(batch, n_heads, n_blocks),
        in_specs=[cols, all_rows, all_cols, rows, cols,
                  pl.BlockSpec((1, 1, nb_pad, HEAD_DIM), lambda b, h, i: (b, h, 0, 0))],
        out_specs=pl.BlockSpec((MOBA_BLOCK, HEAD_DIM), lambda b, h, i: (b * n_blocks + i, h)),
        out_shape=jax.ShapeDtypeStruct((batch * seq, n_heads * HEAD_DIM), F32),
        scratch_shapes=[pltpu.VMEM((1, MOBA_BLOCK), F32), pltpu.VMEM((1, MOBA_BLOCK), F32),
                        pltpu.VMEM((HEAD_DIM, MOBA_BLOCK), F32), pltpu.VMEM((nb_pad, MOBA_BLOCK), F32)],
        compiler_params=_params(3),
        name="moba_attention",
    )(q_t, k_hm, v_t, k_hm, v_t, kmean)


def _page_select_body(pt_ref, q_ref, *refs, n_heads, n_past_blocks, pages_per_step):
    page_refs, (sel_ref, km_ref) = refs[:pages_per_step], refs[pages_per_step:]
    g = pl.program_id(1)

    @pl.when(g == 0)
    def _():
        km_ref[...] = jnp.zeros(km_ref.shape, F32)

    blocks_per_step = pages_per_step // PAGES_PER_BLOCK
    for r in range(blocks_per_step):
        tot = jnp.zeros((n_heads, HEAD_DIM), F32)
        for half in range(PAGES_PER_BLOCK):
            tot = tot + jnp.sum(page_refs[r * PAGES_PER_BLOCK + half][0, 0], axis=0)
        tot = tot * (1.0 / MOBA_BLOCK)
        for h in range(n_heads):
            km_ref[h, pl.ds(g * blocks_per_step + r, 1), :] = tot[h:h + 1, :]

    @pl.when(g == pl.num_programs(1) - 1)
    def _():
        for h in range(n_heads):
            sl = slice(h * HEAD_DIM, (h + 1) * HEAD_DIM)
            gate = _nt_dot(q_ref[0, :, sl].astype(BF), km_ref[h].astype(BF))
            lane = lax.broadcasted_iota(jnp.int32, gate.shape, 1)
            gate = jnp.where(lane < n_past_blocks, gate, NEG_INF)
            _, picks = _top_k_lane_mask(gate, MOBA_TOPK)
            out = jnp.full(gate.shape, -1.0, F32)
            for t, idx in enumerate(picks):
                out = jnp.where(lane == t, jnp.where(idx < float(LANES), idx, -1.0), out)
            sel_ref[0, h] = out.astype(jnp.int32)


def _page_select(q, cache_k, page_table, layer, *, n_heads):
    batch, dec_seq, d_att = q.shape
    n_pages = page_table.shape[1]
    n_past_blocks = n_pages // PAGES_PER_BLOCK
    assert n_pages % PAGES_PER_BLOCK == 0 and n_past_blocks <= LANES
    pps = _tile(n_pages, PAGES_PER_STEP)
    assert pps % PAGES_PER_BLOCK == 0

    def page_spec(r):
        return pl.BlockSpec((1, 1, PAGE_SIZE, n_heads, HEAD_DIM),
                            lambda b, g, pt: (layer, pt[b * n_pages + g * pps + r], 0, 0, 0))

    grid_spec = pltpu.PrefetchScalarGridSpec(
        num_scalar_prefetch=1,
        grid=(batch, n_pages // pps),
        in_specs=[pl.BlockSpec((1, dec_seq, d_att), lambda b, g, pt: (b, 0, 0))] + [page_spec(r) for r in range(pps)],
        out_specs=pl.BlockSpec((1, n_heads, dec_seq, LANES), lambda b, g, pt: (b, 0, 0, 0)),
        scratch_shapes=[pltpu.VMEM((n_heads, LANES, HEAD_DIM), F32)],
    )
    return pl.pallas_call(
        functools.partial(_page_select_body, n_heads=n_heads, n_past_blocks=n_past_blocks, pages_per_step=pps),
        grid_spec=grid_spec,
        out_shape=jax.ShapeDtypeStruct((batch, n_heads, dec_seq, LANES), jnp.int32),
        compiler_params=_params(2),
        name="page_select",
    )(page_table.reshape(-1), q, *([cache_k] * pps))


def _page_attend_body(sel_ref, pt_ref, q_ref, kn_ref, vn_ref, ck_ref, cv_ref, o_ref, kbuf, vbuf, sem,
                      *, layer, n_heads, n_pages, dec_seq):
    step = pl.program_id(0)
    n_pairs = dec_seq * MOBA_TOPK
    rows_per_query = MOBA_TOPK * MOBA_BLOCK

    def copies(s, slot):
        b, h = s // n_heads, s % n_heads
        out = []
        for n in range(n_pairs):
            blk = jnp.maximum(sel_ref[s * n_pairs + n], 0)
            for half in range(PAGES_PER_BLOCK):
                page = pt_ref[b * n_pages + blk * PAGES_PER_BLOCK + half]
                rows = pl.ds(n * MOBA_BLOCK + half * PAGE_SIZE, PAGE_SIZE)
                out.append(pltpu.make_async_copy(ck_ref.at[layer, page, :, h, :], kbuf.at[slot, rows, :],
                                                 sem.at[0, slot]))
                out.append(pltpu.make_async_copy(cv_ref.at[layer, page, :, h, :], vbuf.at[slot, rows, :],
                                                 sem.at[1, slot]))
        return out

    @pl.when(step == 0)
    def _():
        for c in copies(0, 0):
            c.start()

    @pl.when(step + 1 < pl.num_programs(0))
    def _():
        for c in copies(step + 1, (step + 1) % 2):
            c.start()

    slot = step % 2
    for c in copies(step, slot):
        c.wait()

    q_pad = jnp.concatenate([q_ref[0], jnp.zeros((LANES - dec_seq, HEAD_DIM), F32)], axis=0).astype(BF)
    s_past = _nt_dot(kbuf[slot].astype(BF), q_pad)
    key = lax.broadcasted_iota(jnp.int32, s_past.shape, 0)
    qry = lax.broadcasted_iota(jnp.int32, s_past.shape, 1)
    first = qry * rows_per_query
    s_past = jnp.where(jnp.logical_and(key >= first, key < first + rows_per_query), s_past, NEG_INF)
    s_own = _nt_dot(kn_ref[0].astype(BF), q_pad)
    key_o = lax.broadcasted_iota(jnp.int32, s_own.shape, 0)
    qry_o = lax.broadcasted_iota(jnp.int32, s_own.shape, 1)
    s_own = jnp.where(key_o <= qry_o, s_own, NEG_INF)
    m = jnp.maximum(jnp.max(s_past, axis=0, keepdims=True), jnp.max(s_own, axis=0, keepdims=True))
    p_past = jnp.exp(s_past - m)
    p_own = jnp.exp(s_own - m)
    inv_l = 1.0 / (jnp.sum(p_past, axis=0, keepdims=True) + jnp.sum(p_own, axis=0, keepdims=True))

    v_past, v_own = vbuf[slot], vn_ref[0]
    out_rows = []
    for i in range(dec_seq):
        span = slice(i * rows_per_query, (i + 1) * rows_per_query)
        acc = jnp.sum(p_past[span, i:i + 1] * v_past[span, :], axis=0, keepdims=True)
        acc = acc + jnp.sum(p_own[:, i:i + 1] * v_own, axis=0, keepdims=True)
        out_rows.append(acc * inv_l[:, i:i + 1])
    o_ref[0] = jnp.concatenate(out_rows, axis=0)


def _page_attend(sel, page_table, q, k_new, v_new, cache_k, cache_v, layer, *, n_heads):
    batch, dec_seq, d_att = q.shape
    n_pages = page_table.shape[1]
    assert n_pages // PAGES_PER_BLOCK >= MOBA_TOPK and dec_seq <= SUBLANES
    n_rows = dec_seq * MOBA_TOPK * MOBA_BLOCK
    head = pl.BlockSpec((1, dec_seq, HEAD_DIM), lambda s, s_, p_: (s // n_heads, 0, s % n_heads))
    grid_spec = pltpu.PrefetchScalarGridSpec(
        num_scalar_prefetch=2,
        grid=(batch * n_heads,),
        in_specs=[head, head, head, pl.BlockSpec(memory_space=pl.ANY), pl.BlockSpec(memory_space=pl.ANY)],
        out_specs=head,
        scratch_shapes=[pltpu.VMEM((2, n_rows, HEAD_DIM), F32), pltpu.VMEM((2, n_rows, HEAD_DIM), F32),
                        pltpu.SemaphoreType.DMA((2, 2))],
    )
    return pl.pallas_call(
        functools.partial(_page_attend_body, layer=layer, n_heads=n_heads, n_pages=n_pages, dec_seq=dec_seq),
        grid_spec=grid_spec,
        out_shape=jax.ShapeDtypeStruct((batch, dec_seq, d_att), F32),
        compiler_params=_params(1),
        name="page_attend",
    )(sel, page_table.reshape(-1), q, k_new, v_new, cache_k, cache_v)


def _shift_rows(x, d, fill):
    row = lax.broadcasted_iota(jnp.int32, x.shape, 0)
    return jnp.where(row >= d, pltpu.roll(x, d, axis=0), fill)


def _lru_body(xl_ref, gl_ref, h0_ref, c0_ref, cw_ref, cb_ref, wa_ref, ba_ref, wx_ref, bx_ref, lam_ref, gol_ref,
              y_ref, hlast_ref, cnew_ref, xp_ref, h_ref, *, tc, n_blocks):
    step = pl.program_id(1)
    pad = SUBLANES
    hist = CONV_W - 1

    @pl.when(step == 0)
    def _():
        xp_ref[...] = jnp.zeros(xp_ref.shape, F32)
        xp_ref[pad - hist:pad, :] = c0_ref[0]
        h_ref[...] = h0_ref[0]

    xp_ref[pad:pad + tc, :] = xl_ref[...]
    xc = cb_ref[0]
    for j in range(CONV_W):
        xc = xc + xp_ref[pad - hist + j:pad - hist + j + tc, :] * cw_ref[0, j:j + 1, :]
    new_hist = xp_ref[pad + tc - hist:pad + tc, :]
    xp_ref[pad - hist:pad, :] = new_hist

    xcb = xc.astype(BF)
    ra, ix = [], []
    for n in range(n_blocks):
        sl = slice(n * LRU_BLOCK, (n + 1) * LRU_BLOCK)
        ra.append(_dot(xcb[:, sl], wa_ref[0, n].astype(BF)))
        ix.append(_dot(xcb[:, sl], wx_ref[0, n].astype(BF)))
    r = jax.nn.sigmoid(jnp.concatenate(ra, axis=-1) + ba_ref[0])
    i = jax.nn.sigmoid(jnp.concatenate(ix, axis=-1) + bx_ref[0])
    neg_lam = -lam_ref[0]
    softplus = jnp.maximum(neg_lam, 0.0) + jnp.log1p(jnp.exp(-jnp.abs(neg_lam)))
    log_a = (-LRU_C) * r * softplus
    a = jnp.exp(log_a)
    u = jnp.sqrt(-jnp.tanh(log_a) * (a * a + 1.0)) * (i * xc)

    d = 1
    while d < tc:
        a_prev = _shift_rows(a, d, 1.0)
        u_prev = _shift_rows(u, d, 0.0)
        u = a * u_prev + u
        a = a * a_prev
        d *= 2
    h = a * h_ref[...] + u
    h_ref[...] = h[tc - 1:tc, :]

    gl = gl_ref[...]
    gelu = 0.5 * gl * (1.0 + jnp.tanh(0.7978845608028654 * (gl + 0.044715 * (gl * gl * gl))))
    yl = h * gelu
    ms = jnp.mean(yl * yl, axis=-1, keepdims=True)
    y_ref[...] = yl * lax.rsqrt(ms + EPS) * gol_ref[0]

    @pl.when(step == pl.num_programs(1) - 1)
    def _():
        hlast_ref[0] = h[tc - 1:tc, :]
        cnew_ref[0] = new_hist


def _lru_mixer(proj, h0, conv0, conv_w, conv_b, w_a, b_a, w_x, b_x, lam, g_ol, layer, *, batch, seq, d_att, d_lru):
    assert d_att == d_lru
    m = proj.shape[0]
    tc = _tile(seq, T_LRU)
    per_b = seq // tc
    n_blocks = d_lru // LRU_BLOCK
    depth = conv_b.shape[0]
    vec = lambda a: a.reshape(depth, 1, d_lru)
    vec_spec = pl.BlockSpec((1, 1, d_lru), lambda b, s: (layer, 0, 0))
    gate_w = pl.BlockSpec((1, n_blocks, LRU_BLOCK, LRU_BLOCK), lambda b, s: (layer, 0, 0, 0))
    state = pl.BlockSpec((1, 1, d_lru), lambda b, s: (b, 0, 0))
    hist = pl.BlockSpec((1, CONV_W - 1, d_lru), lambda b, s: (b, 0, 0))
    return pl.pallas_call(
        functools.partial(_lru_body, tc=tc, n_blocks=n_blocks),
        grid=(batch, per_b),
        in_specs=[pl.BlockSpec((tc, d_lru), lambda b, s: (b * per_b + s, 3)),
                  pl.BlockSpec((tc, d_lru), lambda b, s: (b * per_b + s, 4)),
                  state, hist,
                  pl.BlockSpec((1, CONV_W, d_lru), lambda b, s: (layer, 0, 0)), vec_spec,
                  gate_w, vec_spec, gate_w, vec_spec, vec_spec, vec_spec],
        out_specs=[pl.BlockSpec((tc, d_lru), lambda b, s: (b * per_b + s, 0)), state, hist],
        out_shape=[jax.ShapeDtypeStruct((m, d_lru), F32),
                   jax.ShapeDtypeStruct((batch, 1, d_lru), F32),
                   jax.ShapeDtypeStruct((batch, CONV_W - 1, d_lru), F32)],
        scratch_shapes=[pltpu.VMEM((tc + SUBLANES, d_lru), F32), pltpu.VMEM((1, d_lru), F32)],
        compiler_params=_params(2),
        name="lru_mixer",
    )(proj, proj, h0.reshape(batch, 1, d_lru), conv0, conv_w, vec(conv_b), w_a, vec(b_a), w_x, vec(b_x),
      vec(lam), vec(g_ol))


def _out_proj_body(o_ref, yl_ref, goa_ref, w_ref, x_ref, gt_ref, xo_ref, wbf_ref, *, d_att):
    @pl.when(pl.program_id(1) == 0)
    def _():
        wbf_ref[...] = w_ref[0].astype(BF)

    o = o_ref[...]
    ms = jnp.mean(o * o, axis=-1, keepdims=True)
    ya = (o * lax.rsqrt(ms + EPS) * goa_ref[0]).astype(BF)
    y = _dot(ya, wbf_ref[0:d_att, :]) + _dot(yl_ref[...].astype(BF), wbf_ref[d_att:, :])
    xo_ref[...] = x_ref[...] + gt_ref[0] * y


def _out_proj(o_att, y_lru, g_oa, w_out, x, mod_g, layer, *, gate_chunk):
    m, d = x.shape
    d_att, d_lru = o_att.shape[1], y_lru.shape[1]
    groups, r, _ = mod_g.shape
    rows_per_group = m // groups
    tm = _tile(rows_per_group, TM_MATMUL // 2)
    assert r in (1, tm)
    per_group = rows_per_group // tm
    tn = _tile(d, TN_MATMUL)
    n_col = d // tn
    return pl.pallas_call(
        functools.partial(_out_proj_body, d_att=d_att),
        grid=(n_col, m // tm),
        in_specs=[pl.BlockSpec((tm, d_att), lambda j, i: (i, 0)),
                  pl.BlockSpec((tm, d_lru), lambda j, i: (i, 0)),
                  pl.BlockSpec((1, 1, d_att), lambda j, i: (layer, 0, 0)),
                  pl.BlockSpec((1, d, tn), lambda j, i: (layer, 0, j)),
                  pl.BlockSpec((tm, tn), lambda j, i: (i, j)),
                  pl.BlockSpec((1, r, tn), lambda j, i: (i // per_group, 0, gate_chunk * n_col + j))],
        out_specs=pl.BlockSpec((tm, tn), lambda j, i: (i, j)),
        out_shape=jax.ShapeDtypeStruct((m, d), F32),
        scratch_shapes=[pltpu.VMEM((d, tn), BF)],
        compiler_params=_params(2),
        name="out_proj",
    )(o_att, y_lru, g_oa.reshape(-1, 1, d_att), w_out, x, mod_g)


def _swiglu_partial(x, w1_ref, w3_ref, w2_ref):
    h1 = _dot(x, w1_ref.astype(BF))
    h3 = _dot(x, w3_ref.astype(BF))
    hidden = (h1 * jax.nn.sigmoid(h1) * h3).astype(BF)
    return _dot(hidden, w2_ref.astype(BF))


def _ffn_body(x_ref, w1_ref, w3_ref, w2_ref, o_ref):
    @pl.when(pl.program_id(1) == 0)
    def _():
        o_ref[...] = jnp.zeros(o_ref.shape, F32)

    o_ref[...] += _swiglu_partial(x_ref[...], w1_ref[0], w3_ref[0], w2_ref[0])


def _dense_ffn(xn, w1, w3, w2, idx):
    m, d = xn.shape
    d_ff = w1.shape[2]
    tm, tf = _tile(m, TM_FFN), _tile(d_ff, TF_FFN)
    return pl.pallas_call(
        _ffn_body,
        grid=(m // tm, d_ff // tf),
        in_specs=[pl.BlockSpec((tm, d), lambda i, f: (i, 0)),
                  pl.BlockSpec((1, d, tf), lambda i, f: (idx, 0, f)),
                  pl.BlockSpec((1, d, tf), lambda i, f: (idx, 0, f)),
                  pl.BlockSpec((1, tf, d), lambda i, f: (idx, f, 0))],
        out_specs=pl.BlockSpec((tm, d), lambda i, f: (i, 0)),
        out_shape=jax.ShapeDtypeStruct((m, d), F32),
        compiler_params=_params(2),
        name="dense_ffn",
    )(xn, w1, w3, w2)


def _router_body(x_ref, w_ref, idx_ref, p_ref, *, n_experts):
    logits = _dot(x_ref[...].astype(BF), w_ref[0].astype(BF))
    lane = lax.broadcasted_iota(jnp.int32, logits.shape, 1)
    logits = jnp.where(lane < n_experts, logits, NEG_INF)
    top = []
    g = logits
    lane_f = lane.astype(F32)
    for _ in range(TOP_K):
        mx = jnp.max(g, axis=-1, keepdims=True)
        idx = jnp.min(jnp.where(g == mx, lane_f, float(LANES)), axis=-1, keepdims=True)
        g = jnp.where(lane_f == idx, NEG_INF, g)
        top.append((mx, idx))
    (v0, i0), (v1, i1) = top
    e1 = jnp.exp(v1 - v0)
    p0 = 1.0 / (1.0 + e1)
    p1 = e1 / (1.0 + e1)
    idx_ref[...] = jnp.where(lane == 0, i0, jnp.where(lane == 1, i1, 0.0)).astype(jnp.int32)
    p_ref[...] = jnp.where(lane == 0, p0, jnp.where(lane == 1, p1, 0.0))


def _router(xn, router_w_pad, idx, n_experts):
    m, d = xn.shape
    tm = _tile(m, TM_NORM)
    row = pl.BlockSpec((tm, LANES), lambda i: (i, 0))
    return pl.pallas_call(
        functools.partial(_router_body, n_experts=n_experts),
        grid=(m // tm,),
        in_specs=[pl.BlockSpec((tm, d), lambda i: (i, 0)), pl.BlockSpec((1, d, LANES), lambda i: (idx, 0, 0))],
        out_specs=[row, row],
        out_shape=[jax.ShapeDtypeStruct((m, LANES), jnp.int32), jax.ShapeDtypeStruct((m, LANES), F32)],
        compiler_params=_params(1),
        name="router",
    )(xn, router_w_pad)


def _expert_body(te_ref, used_ref, tok_ref, x_hbm, w1_ref, w3_ref, w2_ref, p_ref, o_ref, xs_ref, xbf_ref, sem,
                 *, tm):
    t, f = pl.program_id(0), pl.program_id(1)
    active = t < used_ref[0]

    def row_copy(r):
        tok = tok_ref[t * tm + r]
        return pltpu.make_async_copy(x_hbm.at[pl.ds(tok, 1), :], xs_ref.at[pl.ds(r, 1), :], sem.at[0])

    @pl.when(jnp.logical_and(active, f == 0))
    def _():
        def issue(r, c):
            row_copy(r).start()
            return c

        def drain(r, c):
            row_copy(r).wait()
            return c

        lax.fori_loop(0, tm, issue, 0)
        lax.fori_loop(0, tm, drain, 0)
        xbf_ref[...] = xs_ref[...].astype(BF)

    @pl.when(f == 0)
    def _():
        o_ref[...] = jnp.zeros(o_ref.shape, F32)

    @pl.when(active)
    def _():
        o_ref[...] += _swiglu_partial(xbf_ref[...], w1_ref[0, 0], w3_ref[0, 0], w2_ref[0, 0])

    @pl.when(jnp.logical_and(active, f == pl.num_programs(1) - 1))
    def _():
        o_ref[...] *= p_ref[...]


def _expert_ffn(xn, w1, w3, w2, idx, tile_expert, n_used, slot_token, slot_p, *, tm):
    m, d = xn.shape
    d_ff = w1.shape[3]
    n_tiles = tile_expert.shape[0]
    tf = _tile(d_ff, TF_FFN)
    n_f = d_ff // tf
    assert n_f > 1

    def fcol(t, f, used):
        return jnp.where(t < used[0], f, n_f - 1)

    grid_spec = pltpu.PrefetchScalarGridSpec(
        num_scalar_prefetch=3,
        grid=(n_tiles, n_f),
        in_specs=[pl.BlockSpec(memory_space=pl.ANY),
                  pl.BlockSpec((1, 1, d, tf), lambda t, f, te, used, tok: (idx, te[t], 0, fcol(t, f, used))),
                  pl.BlockSpec((1, 1, d, tf), lambda t, f, te, used, tok: (idx, te[t], 0, fcol(t, f, used))),
                  pl.BlockSpec((1, 1, tf, d), lambda t, f, te, used, tok: (idx, te[t], fcol(t, f, used), 0)),
                  pl.BlockSpec((tm, 1), lambda t, f, te, used, tok: (t, 0))],
        out_specs=pl.BlockSpec((tm, d), lambda t, f, te, used, tok: (t, 0)),
        scratch_shapes=[pltpu.VMEM((tm, d), F32), pltpu.VMEM((tm, d), BF), pltpu.SemaphoreType.DMA((1,))],
    )
    return pl.pallas_call(
        functools.partial(_expert_body, tm=tm),
        grid_spec=grid_spec,
        out_shape=jax.ShapeDtypeStruct((n_tiles * tm, d), F32),
        compiler_params=_params(2),
        name="expert_ffn",
    )(tile_expert, n_used, slot_token, xn, w1, w3, w2, slot_p)


def _combine_body(pos_ref, ys_hbm, o_ref, a_ref, b_ref, sem, *, tm):
    base = pl.program_id(0) * tm

    def copies(r):
        s0 = pos_ref[(base + r) * TOP_K]
        s1 = pos_ref[(base + r) * TOP_K + 1]
        return (pltpu.make_async_copy(ys_hbm.at[pl.ds(s0, 1), :], a_ref.at[pl.ds(r, 1), :], sem.at[0]),
                pltpu.make_async_copy(ys_hbm.at[pl.ds(s1, 1), :], b_ref.at[pl.ds(r, 1), :], sem.at[1]))

    def issue(r, c):
        for cp in copies(r):
            cp.start()
        return c

    def drain(r, c):
        for cp in copies(r):
            cp.wait()
        return c

    lax.fori_loop(0, tm, issue, 0)
    lax.fori_loop(0, tm, drain, 0)
    o_ref[...] = a_ref[...] + b_ref[...]


def _combine(ys, pos, m):
    d = ys.shape[1]
    tm = _tile(m, TM_COMBINE)
    grid_spec = pltpu.PrefetchScalarGridSpec(
        num_scalar_prefetch=1,
        grid=(m // tm,),
        in_specs=[pl.BlockSpec(memory_space=pl.ANY)],
        out_specs=pl.BlockSpec((tm, d), lambda i, pos_: (i, 0)),
        scratch_shapes=[pltpu.VMEM((tm, d), F32), pltpu.VMEM((tm, d), F32), pltpu.SemaphoreType.DMA((2,))],
    )
    return pl.pallas_call(
        functools.partial(_combine_body, tm=tm),
        grid_spec=grid_spec,
        out_shape=jax.ShapeDtypeStruct((m, d), F32),
        compiler_params=_params(1),
        name="moe_combine",
    )(pos, ys)


def _moe_ffn(xns, router_w_pad, w1, w3, w2, idx, n_experts):
    routed = [_router(xn, router_w_pad, idx, n_experts) for xn in xns]
    e_flat = jnp.concatenate([r[0][:, :TOP_K].reshape(-1) for r in routed])
    p_flat = jnp.concatenate([r[1][:, :TOP_K].reshape(-1) for r in routed])
    xn = jnp.concatenate(xns, axis=0) if len(xns) > 1 else xns[0]
    m = xn.shape[0]
    n_assign = m * TOP_K
    tm = min(TM_FFN, m)
    n_tiles = -(-n_assign // tm) + n_experts
    one_hot = (e_flat[:, None] == jnp.arange(n_experts, dtype=jnp.int32)[None, :]).astype(jnp.int32)
    csum = jnp.cumsum(one_hot, axis=0)
    rank = jnp.sum(csum * one_hot, axis=1) - 1
    counts = csum[-1]
    tiles_e = (counts + tm - 1) // tm
    tile_end = jnp.cumsum(tiles_e)
    tile_start = tile_end - tiles_e
    slot = (jnp.sum(tile_start[None, :] * one_hot, axis=1) * tm + rank).astype(jnp.int32)
    slot_token = jnp.zeros((n_tiles * tm,), jnp.int32).at[slot].set(jnp.arange(n_assign, dtype=jnp.int32) // TOP_K)
    slot_p = jnp.zeros((n_tiles * tm,), F32).at[slot].set(p_flat).reshape(-1, 1)
    n_used = tile_end[-1]
    t_ids = jnp.minimum(jnp.arange(n_tiles, dtype=jnp.int32), n_used - 1)
    tile_expert = jnp.sum((t_ids[:, None] >= tile_end[None, :]).astype(jnp.int32), axis=1).astype(jnp.int32)
    ys = _expert_ffn(xn, w1, w3, w2, idx, tile_expert, n_used.reshape(1).astype(jnp.int32), slot_token, slot_p, tm=tm)
    outs, start = [], 0
    for g in xns:
        rows = g.shape[0]
        outs.append(_combine(ys, lax.slice(slot, (start * TOP_K,), ((start + rows) * TOP_K,)), rows))
        start += rows
    return outs


def _rope_tables(pos):
    half = HEAD_DIM // 2
    inv = ROPE_THETA ** (-(jnp.arange(half, dtype=F32) / half))
    ang = pos.astype(F32)[:, None] * inv[None, :]
    cos, sin = jnp.cos(ang), jnp.sin(ang)
    return jnp.concatenate([cos, cos], axis=-1), jnp.concatenate([-sin, sin], axis=-1)


class _Trunk:
    def __init__(self, x3, mod_rows, pos, h_init, conv_init, paged, depth):
        self.batch, self.seq, self.d = x3.shape
        self.x = x3.reshape(self.batch * self.seq, self.d)
        self.cos, self.sin_signed = _rope_tables(pos)
        self.h_init, self.conv_init, self.paged = h_init, conv_init, paged
        if self.seq % TM_NORM == 0:
            self.mods = [mod_rows[l][:, None, :] for l in range(depth)]
        else:
            self.mods = [jnp.repeat(mod_rows[l], self.seq, axis=0)[None] for l in range(depth)]
        self.f = None
        self.k_rows, self.v_rows, self.h_fin, self.conv_fin = [], [], [], []

    def mix(self, l, w, moe_layer):
        batch, seq, mods = self.batch, self.seq, self.mods
        m = batch * seq
        d_att = w["g_out_att"].shape[1]
        d_lru = w["g_out_lru"].shape[1]
        n_heads = d_att // HEAD_DIM
        cos, sin_signed, paged = self.cos, self.sin_signed, self.paged
        if l == 0:
            x = self.x
            _, xn = _resid_norm(x, g_norm=w["g_norm_mix"], layer=l, norm_mod=mods[l], scale_chunk=1, shift_chunk=0)
        else:
            x, xn = _resid_norm(self.x, f=self.f, gate_mod=mods[l - 1], gate_chunk=5, g_norm=w["g_norm_mix"],
                                layer=l, norm_mod=mods[l], scale_chunk=1, shift_chunk=0)
        proj = _in_proj(xn, w["w_in"], l)
        if paged is None:
            k_new, v_new, q_t, k_hm, v_t, kmean = _qkv_prep(
                proj, w["g_q"], w["g_k"], cos, sin_signed, l, batch=batch, seq=seq, d_att=d_att, head_major=True)
            n_blocks = seq // MOBA_BLOCK
            km = kmean.reshape(batch, n_blocks, n_heads, HEAD_DIM).transpose(0, 2, 1, 3)
            km = jnp.pad(km, ((0, 0), (0, 0), (0, (-n_blocks) % SUBLANES), (0, 0)))
            o_att = _moba_attention(q_t, k_hm, v_t, km)
        else:
            cache_k, cache_v, page_table = paged
            k_new, v_new, q = _qkv_prep(
                proj, w["g_q"], w["g_k"], cos, sin_signed, l, batch=batch, seq=seq, d_att=d_att, head_major=False)
            q3 = q.reshape(batch, seq, d_att)
            sel = _page_select(q3, cache_k, page_table, l, n_heads=n_heads)
            sel_flat = sel[..., :MOBA_TOPK].reshape(-1)
            o_att = _page_attend(sel_flat, page_table, q3, k_new.reshape(batch, seq, d_att),
                                 v_new.reshape(batch, seq, d_att), cache_k, cache_v, l,
                                 n_heads=n_heads).reshape(m, d_att)
        y_lru, h_last, conv_new = _lru_mixer(
            proj, self.h_init[l], self.conv_init[l], w["conv_w"], w["conv_b"], w["w_gate_a"], w["b_gate_a"],
            w["w_gate_x"], w["b_gate_x"], w["lru_lambda"], w["g_out_lru"], l,
            batch=batch, seq=seq, d_att=d_att, d_lru=d_lru)
        self.x = _out_proj(o_att, y_lru, w["g_out_att"], w["w_out"], x, mods[l], l, gate_chunk=2)
        self.k_rows.append(k_new)
        self.v_rows.append(v_new)
        self.h_fin.append(h_last)
        self.conv_fin.append(conv_new)
        _, xn2 = _resid_norm(self.x, g_norm=w["g_norm_ffn"], layer=l, norm_mod=mods[l], scale_chunk=4,
                             shift_chunk=3, xn_dtype=F32 if moe_layer else BF)
        return xn2

    def finish(self, w):
        depth = len(self.k_rows)
        d_att = w["g_out_att"].shape[1]
        d_lru = w["g_out_lru"].shape[1]
        x, _ = _resid_norm(self.x, f=self.f, gate_mod=self.mods[depth - 1], gate_chunk=5)
        shape_kv = (depth, self.batch, self.seq, d_att // HEAD_DIM, HEAD_DIM)
        return (x.reshape(self.batch, self.seq, self.d), jnp.stack(self.k_rows).reshape(shape_kv),
                jnp.stack(self.v_rows).reshape(shape_kv),
                jnp.stack(self.h_fin).reshape(depth, self.batch, d_lru), jnp.stack(self.conv_fin))


def kernel(x_prompt, x_sample, cache_k, cache_v, state_lru_h, state_lru_conv, page_table, c_prompt, c_sample,
           g_norm_mix, g_norm_ffn, w_mod, b_mod, w_in, g_q, g_k, conv_w, conv_b, w_gate_a, b_gate_a, w_gate_x,
           b_gate_x, lru_lambda, g_out_att, g_out_lru, w_out, ffn_w1, ffn_w3, ffn_w2, router_w, moe_w1, moe_w3,
           moe_w2):
    depth, d = g_norm_mix.shape
    w = dict(g_norm_mix=g_norm_mix.reshape(depth, 1, d), g_norm_ffn=g_norm_ffn.reshape(depth, 1, d), w_in=w_in,
             g_q=g_q, g_k=g_k, conv_w=conv_w, conv_b=conv_b, w_gate_a=w_gate_a, b_gate_a=b_gate_a,
             w_gate_x=w_gate_x, b_gate_x=b_gate_x, lru_lambda=lru_lambda, g_out_att=g_out_att,
             g_out_lru=g_out_lru, w_out=w_out, ffn_w1=ffn_w1, ffn_w3=ffn_w3, ffn_w2=ffn_w2, router_w=router_w,
             moe_w1=moe_w1, moe_w3=moe_w3, moe_w2=moe_w2)
    b_p, s_p, _ = x_prompt.shape
    b_s, s_s, _ = x_sample.shape
    d_att = g_out_att.shape[1]
    d_lru = g_out_lru.shape[1]

    n_cond = b_p + b_s
    pad_rows = (-n_cond) % (2 * SUBLANES)
    c_all = jnp.concatenate([c_prompt, c_sample, jnp.zeros((pad_rows, d), F32)], axis=0)
    mod_all = _modulation(c_all, w_mod, b_mod)

    past_len = page_table.shape[1] * PAGE_SIZE
    pos_s = past_len + jnp.arange(s_s, dtype=jnp.int32)
    sample = _Trunk(x_sample, mod_all[:, b_p:n_cond], pos_s, state_lru_h, state_lru_conv,
                    (cache_k, cache_v, page_table), depth)
    pos_p = jnp.arange(s_p, dtype=jnp.int32)
    prompt = _Trunk(x_prompt, mod_all[:, :b_p], pos_p, jnp.zeros((depth, b_p, d_lru), F32),
                    jnp.zeros((depth, b_p, CONV_W - 1, d_lru), F32), None, depth)
    trunks = (sample, prompt)

    n_experts = router_w.shape[2]
    router_w_pad = jnp.pad(router_w, ((0, 0), (0, 0), (0, LANES - n_experts)))
    for l in range(depth):
        moe_layer = l % 2 == 1
        xn2 = [t.mix(l, w, moe_layer) for t in trunks]
        if moe_layer:
            fs = _moe_ffn(xn2, router_w_pad, moe_w1, moe_w3, moe_w2, l // 2, n_experts)
        else:
            fs = [_dense_ffn(xn, ffn_w1, ffn_w3, ffn_w2, l // 2) for xn in xn2]
        for t, f in zip(trunks, fs):
            t.f = f
    y_s, k_s, v_s, h_s, conv_s = sample.finish(w)
    y_p, k_p, v_p, h_p, conv_p = prompt.finish(w)
    return (y_p, y_s, k_p, v_p, h_p, conv_p, k_s, v_s, h_s, conv_s)
```

```python
import functools

import jax
import jax.numpy as jnp
from jax import lax
from jax.experimental import pallas as pl
from jax.experimental.pallas import tpu as pltpu

BF = jnp.bfloat16
F32 = jnp.float32

HEAD_DIM = 128
PAGE_SIZE = 128
MOBA_BLOCK = 256
MOBA_TOPK = 3
PAGES_PER_BLOCK = MOBA_BLOCK // PAGE_SIZE
CONV_W = 4
LRU_C = 8.0
LRU_BLOCK = 128
TOP_K = 2
ROPE_THETA = 10000.0
EPS = 1e-6
LANES = 128
SUBLANES = 8
NEG_INF = float("-inf")

TM_MATMUL = 1024
TN_MATMUL = 1024
TM_NORM = 256
TM_FFN = 1024
TF_FFN = 256
T_LRU = 256
TM_COMBINE = 256
TN_MOD = 1024
PAGES_PER_STEP = 8
HEADS_PER_STEP = 4
VMEM_LIMIT = 56 << 20


def _params(n_axes, vmem=VMEM_LIMIT):
    return pltpu.CompilerParams(dimension_semantics=("arbitrary",) * n_axes, vmem_limit_bytes=vmem)


def _tile(n, pref):
    t = min(n, pref)
    assert n % t == 0, (n, pref)
    return t


def _nt_dot(a, b):
    return lax.dot_general(a, b, (((1,), (1,)), ((), ())), preferred_element_type=F32)


def _dot(a, b):
    return jnp.dot(a, b, preferred_element_type=F32)


def _mod_body(c_ref, w_ref, b_ref, o_ref):
    c = c_ref[...]
    cond = (c * jax.nn.sigmoid(c)).astype(BF)
    o_ref[0] = _dot(cond, w_ref[0].astype(BF)) + b_ref[0]


def _modulation(c_all, w_mod, b_mod):
    depth, d, n = w_mod.shape
    rows = c_all.shape[0]
    tn = _tile(n, TN_MOD)
    return pl.pallas_call(
        _mod_body,
        grid=(depth, n // tn),
        in_specs=[pl.BlockSpec((rows, d), lambda l, j: (0, 0)),
                  pl.BlockSpec((1, d, tn), lambda l, j: (l, 0, j)),
                  pl.BlockSpec((1, 1, tn), lambda l, j: (l, 0, j))],
        out_specs=pl.BlockSpec((1, rows, tn), lambda l, j: (l, 0, j)),
        out_shape=jax.ShapeDtypeStruct((depth, rows, n), F32),
        compiler_params=_params(2),
        name="modulation",
    )(c_all, w_mod, b_mod.reshape(depth, 1, n))


def _resid_norm_body(*refs, has_res, do_norm):
    refs = list(refs)
    x_ref = refs.pop(0)
    x = x_ref[...]
    if has_res:
        f_ref, gt_ref = refs.pop(0), refs.pop(0)
        x = x + gt_ref[0] * f_ref[...]
    if do_norm:
        g_ref, sc_ref, sh_ref = refs.pop(0), refs.pop(0), refs.pop(0)
    if has_res:
        refs.pop(0)[...] = x
    if do_norm:
        ms = jnp.mean(x * x, axis=-1, keepdims=True)
        y = x * lax.rsqrt(ms + EPS) * g_ref[0]
        xn_ref = refs.pop(0)
        xn_ref[...] = (y * (1.0 + sc_ref[0]) + sh_ref[0]).astype(xn_ref.dtype)


def _resid_norm(x, *, f=None, gate_mod=None, gate_chunk=None, g_norm=None, layer=None, norm_mod=None,
                scale_chunk=None, shift_chunk=None, xn_dtype=None):
    m, d = x.shape
    has_res, do_norm = f is not None, g_norm is not None
    groups, r, _ = (gate_mod if has_res else norm_mod).shape
    rows_per_group = m // groups
    tm = _tile(rows_per_group, TM_NORM)
    assert r in (1, tm)
    per_group = rows_per_group // tm

    def mod_spec(chunk):
        return pl.BlockSpec((1, r, d), lambda i: (i // per_group, 0, chunk))

    row_spec = pl.BlockSpec((tm, d), lambda i: (i, 0))
    args, in_specs, out_specs, out_shape = [x], [row_spec], [], []
    if has_res:
        args += [f, gate_mod]
        in_specs += [row_spec, mod_spec(gate_chunk)]
        out_specs.append(row_spec)
        out_shape.append(jax.ShapeDtypeStruct((m, d), F32))
    if do_norm:
        assert norm_mod.shape[:2] == (groups, r)
        args += [g_norm, norm_mod, norm_mod]
        in_specs += [pl.BlockSpec((1, 1, d), lambda i: (layer, 0, 0)), mod_spec(scale_chunk), mod_spec(shift_chunk)]
        out_specs.append(row_spec)
        out_shape.append(jax.ShapeDtypeStruct((m, d), xn_dtype or BF))
    outs = pl.pallas_call(
        functools.partial(_resid_norm_body, has_res=has_res, do_norm=do_norm),
        grid=(m // tm,),
        in_specs=in_specs, out_specs=out_specs, out_shape=out_shape,
        compiler_params=_params(1),
        name="resid_norm",
    )(*args)
    x_new = outs[0] if has_res else x
    xn = outs[-1] if do_norm else None
    return x_new, xn


def _in_proj_body(a_ref, w_ref, o_ref, wbf_ref):
    @pl.when(pl.program_id(1) == 0)
    def _():
        wbf_ref[...] = w_ref[0].astype(BF)

    o_ref[...] = _dot(a_ref[...], wbf_ref[...])


def _in_proj(xn, w_in, layer):
    m, d = xn.shape
    n = w_in.shape[2]
    tm, tn = _tile(m, TM_MATMUL), _tile(n, TN_MATMUL)
    return pl.pallas_call(
        _in_proj_body,
        grid=(n // tn, m // tm),
        in_specs=[pl.BlockSpec((tm, d), lambda j, i: (i, 0)),
                  pl.BlockSpec((1, d, tn), lambda j, i: (layer, 0, j))],
        out_specs=pl.BlockSpec((tm, tn), lambda j, i: (i, j)),
        out_shape=jax.ShapeDtypeStruct((m, n), F32),
        scratch_shapes=[pltpu.VMEM((d, tn), BF)],
        compiler_params=_params(2),
        name="in_proj",
    )(xn, w_in)


def _head_norm_rope(x, g, cos, sin_signed):
    ms = jnp.mean(x * x, axis=-1, keepdims=True)
    y = x * lax.rsqrt(ms + EPS) * g
    return y * cos + pltpu.roll(y, HEAD_DIM // 2, axis=1) * sin_signed


def _prep_body(q_ref, k_ref, v_ref, gq_ref, gk_ref, cos_ref, sin_ref, *out_refs, n_heads, q_scale, head_major):
    cos, sin_signed = cos_ref[...], sin_ref[...]
    gq, gk = gq_ref[0], gk_ref[0]
    if head_major:
        kleaf_ref, vleaf_ref, qt_ref, khm_ref, vt_ref, kmean_ref = out_refs
    else:
        kleaf_ref, vleaf_ref, q_out_ref = out_refs
    v = v_ref[...]
    vleaf_ref[...] = v
    for h in range(n_heads):
        sl = slice(h * HEAD_DIM, (h + 1) * HEAD_DIM)
        qh = _head_norm_rope(q_ref[:, sl], gq, cos, sin_signed) * q_scale
        kh = _head_norm_rope(k_ref[:, sl], gk, cos, sin_signed)
        kleaf_ref[:, sl] = kh
        if head_major:
            qt_ref[0, h] = qh.T.astype(BF)
            khm_ref[0, h] = kh.astype(BF)
            vt_ref[0, h] = v[:, sl].T.astype(BF)
            kmean_ref[0, :, sl] = jnp.mean(kh, axis=0, keepdims=True)
        else:
            q_out_ref[:, sl] = qh


def _qkv_prep(proj, g_q, g_k, cos, sin_signed, layer, *, batch, seq, d_att, head_major):
    m = proj.shape[0]
    n_heads = d_att // HEAD_DIM
    ts = MOBA_BLOCK if head_major else seq
    assert seq % ts == 0
    per_b = seq // ts
    col = lambda c: pl.BlockSpec((ts, d_att), lambda i: (i, c))
    gain = pl.BlockSpec((1, 1, HEAD_DIM), lambda i: (layer, 0, 0))
    table = pl.BlockSpec((ts, HEAD_DIM), lambda i: (i % per_b, 0))
    leaf = pl.BlockSpec((ts, d_att), lambda i: (i, 0))
    leaf_shape = jax.ShapeDtypeStruct((m, d_att), F32)
    if head_major:
        hm = pl.BlockSpec((1, n_heads, ts, HEAD_DIM), lambda i: (i // per_b, 0, i % per_b, 0))
        hm_shape = jax.ShapeDtypeStruct((batch, n_heads, seq, HEAD_DIM), BF)
        hmt = pl.BlockSpec((1, n_heads, HEAD_DIM, ts), lambda i: (i // per_b, 0, 0, i % per_b))
        hmt_shape = jax.ShapeDtypeStruct((batch, n_heads, HEAD_DIM, seq), BF)
        out_specs = [leaf, leaf, hmt, hm, hmt, pl.BlockSpec((1, 1, d_att), lambda i: (i, 0, 0))]
        out_shape = [leaf_shape, leaf_shape, hmt_shape, hm_shape, hmt_shape,
                     jax.ShapeDtypeStruct((m // ts, 1, d_att), F32)]
    else:
        out_specs = [leaf, leaf, leaf]
        out_shape = [leaf_shape, leaf_shape, leaf_shape]
    return pl.pallas_call(
        functools.partial(_prep_body, n_heads=n_heads, q_scale=HEAD_DIM ** -0.5, head_major=head_major),
        grid=(m // ts,),
        in_specs=[col(0), col(1), col(2), gain, gain, table, table],
        out_specs=out_specs, out_shape=out_shape,
        compiler_params=_params(1),
        name="qkv_prep",
    )(proj, proj, proj, g_q.reshape(-1, 1, HEAD_DIM), g_k.reshape(-1, 1, HEAD_DIM), cos, sin_signed)


def _top_k_lane_mask(g, k):
    lane = lax.broadcasted_iota(jnp.int32, g.shape, 1).astype(F32)
    sel = jnp.zeros(g.shape, F32)
    picks = []
    for _ in range(k):
        m = jnp.max(g, axis=-1, keepdims=True)
        hit = jnp.logical_and(g == m, m > NEG_INF)
        idx = jnp.min(jnp.where(hit, lane, float(LANES)), axis=-1, keepdims=True)
        pick = lane == idx
        sel = jnp.where(pick, 1.0, sel)
        g = jnp.where(pick, NEG_INF, g)
        picks.append(idx)
    return sel, picks


def _top_k_row_mask(g, k):
    blk = lax.broadcasted_iota(jnp.int32, g.shape, 0).astype(F32)
    sel = jnp.zeros(g.shape, F32)
    for _ in range(k):
        m = jnp.max(g, axis=0, keepdims=True)
        hit = jnp.logical_and(g == m, m > NEG_INF)
        idx = jnp.min(jnp.where(hit, blk, float(g.shape[0])), axis=0, keepdims=True)
        pick = blk == idx
        sel = jnp.where(pick, 1.0, sel)
        g = jnp.where(pick, NEG_INF, g)
    return sel


def _moba_body(qt_ref, k_ref, vt_ref, kd_ref, vtd_ref, km_ref, o_ref, m_ref, l_ref, acc_ref, sel_ref, *, n_blocks):
    qi = pl.program_id(2)
    n_group = qt_ref.shape[1]

    for g in range(n_group):
        qt = qt_ref[0, g]
        gate = _dot(km_ref[0, g].astype(BF), qt)
        blk = lax.broadcasted_iota(jnp.int32, gate.shape, 0)
        sel_ref[g] = _top_k_row_mask(jnp.where(blk < qi, gate, NEG_INF), MOBA_TOPK)

        s = _dot(kd_ref[0, g], qt)
        key = lax.broadcasted_iota(jnp.int32, s.shape, 0)
        qry = lax.broadcasted_iota(jnp.int32, s.shape, 1)
        s = jnp.where(key <= qry, s, NEG_INF)
        m0 = jnp.max(s, axis=0, keepdims=True)
        p = jnp.exp(s - m0)
        m_ref[g] = m0
        l_ref[g] = jnp.sum(p, axis=0, keepdims=True)
        acc_ref[g] = _dot(vtd_ref[0, g], p.astype(BF))

    for j in range(n_blocks - 1):
        @pl.when(j < qi)
        def _(j=j):
            span = slice(j * MOBA_BLOCK, (j + 1) * MOBA_BLOCK)
            for g in range(n_group):
                qt = qt_ref[0, g]
                sj = _dot(k_ref[0, g, span, :], qt)
                sj = jnp.where(sel_ref[g, j:j + 1, :] > 0.5, sj, NEG_INF)
                m_old = m_ref[g]
                m_new = jnp.maximum(m_old, jnp.max(sj, axis=0, keepdims=True))
                alpha = jnp.exp(m_old - m_new)
                pj = jnp.exp(sj - m_new)
                l_ref[g] = alpha * l_ref[g] + jnp.sum(pj, axis=0, keepdims=True)
                acc_ref[g] = alpha * acc_ref[g] + _dot(vt_ref[0, g, :, span], pj.astype(BF))
                m_ref[g] = m_new

    for g in range(n_group):
        o_ref[:, g * HEAD_DIM:(g + 1) * HEAD_DIM] = (acc_ref[g] / l_ref[g]).T


def _moba_attention(q_t, k_hm, v_t, kmean):
    batch, n_heads, seq, _ = k_hm.shape
    n_blocks = seq // MOBA_BLOCK
    nb_pad = kmean.shape[2]
    hg = _tile(n_heads, HEADS_PER_STEP)
    rows = pl.BlockSpec((1, hg, MOBA_BLOCK, HEAD_DIM), lambda b, h, i: (b, h, i, 0))
    cols = pl.BlockSpec((1, hg, HEAD_DIM, MOBA_BLOCK), lambda b, h, i: (b, h, 0, i))
    all_rows = pl.BlockSpec((1, hg, seq, HEAD_DIM), lambda b, h, i: (b, h, 0, 0))
    all_cols = pl.BlockSpec((1, hg, HEAD_DIM, seq), lambda b, h, i: (b, h, 0, 0))
    return pl.pallas_call(
        functools.partial(_moba_body, n_blocks=n_blocks),
        grid=(batch, n_heads // hg, n_blocks),
        in_specs=[cols, all_rows, all_cols, rows, cols,
                  pl.BlockSpec((1, hg, nb_pad, HEAD_DIM), lambda b, h, i: (b, h, 0, 0))],
        out_specs=pl.BlockSpec((MOBA_BLOCK, hg * HEAD_DIM), lambda b, h, i: (b * n_blocks + i, h)),
        out_shape=jax.ShapeDtypeStruct((batch * seq, n_heads * HEAD_DIM), F32),
        scratch_shapes=[pltpu.VMEM((hg, 1, MOBA_BLOCK), F32), pltpu.VMEM((hg, 1, MOBA_BLOCK), F32),
                        pltpu.VMEM((hg, HEAD_DIM, MOBA_BLOCK), F32), pltpu.VMEM((hg, nb_pad, MOBA_BLOCK), F32)],
        compiler_params=_params(3),
        name="moba_attention",
    )(q_t, k_hm, v_t, k_hm, v_t, kmean)


def _page_select_body(pt_ref, q_ref, *refs, n_heads, n_past_blocks, pages_per_step):
    page_refs, (sel_ref, km_ref) = refs[:pages_per_step], refs[pages_per_step:]
    g = pl.program_id(1)

    @pl.when(g == 0)
    def _():
        km_ref[...] = jnp.zeros(km_ref.shape, F32)

    blocks_per_step = pages_per_step // PAGES_PER_BLOCK
    for r in range(blocks_per_step):
        tot = jnp.zeros((n_heads, HEAD_DIM), F32)
        for half in range(PAGES_PER_BLOCK):
            tot = tot + jnp.sum(page_refs[r * PAGES_PER_BLOCK + half][0, 0], axis=0)
        tot = tot * (1.0 / MOBA_BLOCK)
        for h in range(n_heads):
            km_ref[h, pl.ds(g * blocks_per_step + r, 1), :] = tot[h:h + 1, :]

    @pl.when(g == pl.num_programs(1) - 1)
    def _():
        for h in range(n_heads):
            sl = slice(h * HEAD_DIM, (h + 1) * HEAD_DIM)
            gate = _nt_dot(q_ref[0, :, sl].astype(BF), km_ref[h].astype(BF))
            lane = lax.broadcasted_iota(jnp.int32, gate.shape, 1)
            gate = jnp.where(lane < n_past_blocks, gate, NEG_INF)
            _, picks = _top_k_lane_mask(gate, MOBA_TOPK)
            out = jnp.full(gate.shape, -1.0, F32)
            for t, idx in enumerate(picks):
                out = jnp.where(lane == t, jnp.where(idx < float(LANES), idx, -1.0), out)
            sel_ref[0, h] = out.astype(jnp.int32)


def _page_select(q, cache_k, page_table, layer, *, n_heads):
    batch, dec_seq, d_att = q.shape
    n_pages = page_table.shape[1]
    n_past_blocks = n_pages // PAGES_PER_BLOCK
    assert n_pages % PAGES_PER_BLOCK == 0 and n_past_blocks <= LANES
    pps = _tile(n_pages, PAGES_PER_STEP)
    assert pps % PAGES_PER_BLOCK == 0

    def page_spec(r):
        return pl.BlockSpec((1, 1, PAGE_SIZE, n_heads, HEAD_DIM),
                            lambda b, g, pt: (layer, pt[b * n_pages + g * pps + r], 0, 0, 0))

    grid_spec = pltpu.PrefetchScalarGridSpec(
        num_scalar_prefetch=1,
        grid=(batch, n_pages // pps),
        in_specs=[pl.BlockSpec((1, dec_seq, d_att), lambda b, g, pt: (b, 0, 0))] + [page_spec(r) for r in range(pps)],
        out_specs=pl.BlockSpec((1, n_heads, dec_seq, LANES), lambda b, g, pt: (b, 0, 0, 0)),
        scratch_shapes=[pltpu.VMEM((n_heads, LANES, HEAD_DIM), F32)],
    )
    return pl.pallas_call(
        functools.partial(_page_select_body, n_heads=n_heads, n_past_blocks=n_past_blocks, pages_per_step=pps),
        grid_spec=grid_spec,
        out_shape=jax.ShapeDtypeStruct((batch, n_heads, dec_seq, LANES), jnp.int32),
        compiler_params=_params(2),
        name="page_select",
    )(page_table.reshape(-1), q, *([cache_k] * pps))


def _page_attend_body(sel_ref, pt_ref, q_ref, kn_ref, vn_ref, ck_ref, cv_ref, o_ref, kbuf, vbuf, sem,
                      *, layer, n_heads, n_pages, dec_seq):
    step = pl.program_id(0)
    n_pairs = dec_seq * MOBA_TOPK
    rows_per_query = MOBA_TOPK * MOBA_BLOCK

    def copies(s, slot):
        b, h = s // n_heads, s % n_heads
        out = []
        for n in range(n_pairs):
            blk = jnp.maximum(sel_ref[s * n_pairs + n], 0)
            for half in range(PAGES_PER_BLOCK):
                page = pt_ref[b * n_pages + blk * PAGES_PER_BLOCK + half]
                rows = pl.ds(n * MOBA_BLOCK + half * PAGE_SIZE, PAGE_SIZE)
                out.append(pltpu.make_async_copy(ck_ref.at[layer, page, :, h, :], kbuf.at[slot, rows, :],
                                                 sem.at[0, slot]))
                out.append(pltpu.make_async_copy(cv_ref.at[layer, page, :, h, :], vbuf.at[slot, rows, :],
                                                 sem.at[1, slot]))
        return out

    @pl.when(step == 0)
    def _():
        for c in copies(0, 0):
            c.start()

    @pl.when(step + 1 < pl.num_programs(0))
    def _():
        for c in copies(step + 1, (step + 1) % 2):
            c.start()

    slot = step % 2
    for c in copies(step, slot):
        c.wait()

    q_pad = jnp.concatenate([q_ref[0], jnp.zeros((LANES - dec_seq, HEAD_DIM), F32)], axis=0).astype(BF)
    s_past = _nt_dot(kbuf[slot].astype(BF), q_pad)
    key = lax.broadcasted_iota(jnp.int32, s_past.shape, 0)
    qry = lax.broadcasted_iota(jnp.int32, s_past.shape, 1)
    first = qry * rows_per_query
    s_past = jnp.where(jnp.logical_and(key >= first, key < first + rows_per_query), s_past, NEG_INF)
    s_own = _nt_dot(kn_ref[0].astype(BF), q_pad)
    key_o = lax.broadcasted_iota(jnp.int32, s_own.shape, 0)
    qry_o = lax.broadcasted_iota(jnp.int32, s_own.shape, 1)
    s_own = jnp.where(key_o <= qry_o, s_own, NEG_INF)
    m = jnp.maximum(jnp.max(s_past, axis=0, keepdims=True), jnp.max(s_own, axis=0, keepdims=True))
    p_past = jnp.exp(s_past - m)
    p_own = jnp.exp(s_own - m)
    inv_l = 1.0 / (jnp.sum(p_past, axis=0, keepdims=True) + jnp.sum(p_own, axis=0, keepdims=True))

    v_past, v_own = vbuf[slot], vn_ref[0]
    out_rows = []
    for i in range(dec_seq):
        span = slice(i * rows_per_query, (i + 1) * rows_per_query)
        acc = jnp.sum(p_past[span, i:i + 1] * v_past[span, :], axis=0, keepdims=True)
        acc = acc + jnp.sum(p_own[:, i:i + 1] * v_own, axis=0, keepdims=True)
        out_rows.append(acc * inv_l[:, i:i + 1])
    o_ref[0] = jnp.concatenate(out_rows, axis=0)


def _page_attend(sel, page_table, q, k_new, v_new, cache_k, cache_v, layer, *, n_heads):
    batch, dec_seq, d_att = q.shape
    n_pages = page_table.shape[1]
    assert n_pages // PAGES_PER_BLOCK >= MOBA_TOPK and dec_seq <= SUBLANES
    n_rows = dec_seq * MOBA_TOPK * MOBA_BLOCK
    head = pl.BlockSpec((1, dec_seq, HEAD_DIM), lambda s, s_, p_: (s // n_heads, 0, s % n_heads))
    grid_spec = pltpu.PrefetchScalarGridSpec(
        num_scalar_prefetch=2,
        grid=(batch * n_heads,),
        in_specs=[head, head, head, pl.BlockSpec(memory_space=pl.ANY), pl.BlockSpec(memory_space=pl.ANY)],
        out_specs=head,
        scratch_shapes=[pltpu.VMEM((2, n_rows, HEAD_DIM), F32), pltpu.VMEM((2, n_rows, HEAD_DIM), F32),
                        pltpu.SemaphoreType.DMA((2, 2))],
    )
    return pl.pallas_call(
        functools.partial(_page_attend_body, layer=layer, n_heads=n_heads, n_pages=n_pages, dec_seq=dec_seq),
        grid_spec=grid_spec,
        out_shape=jax.ShapeDtypeStruct((batch, dec_seq, d_att), F32),
        compiler_params=_params(1),
        name="page_attend",
    )(sel, page_table.reshape(-1), q, k_new, v_new, cache_k, cache_v)


def _shift_rows(x, d, fill):
    if d % SUBLANES == 0:
        return jnp.concatenate([jnp.full((d, x.shape[1]), fill, x.dtype), x[:x.shape[0] - d]], axis=0)
    row = lax.broadcasted_iota(jnp.int32, x.shape, 0)
    return jnp.where(row >= d, pltpu.roll(x, d, axis=0), fill)


def _lru_body(xl_ref, gl_ref, h0_ref, c0_ref, cw_ref, cb_ref, wa_ref, ba_ref, wx_ref, bx_ref, lam_ref, gol_ref,
              y_ref, hlast_ref, cnew_ref, xp_ref, h_ref, *, tc, n_blocks):
    step = pl.program_id(1)
    pad = SUBLANES
    hist = CONV_W - 1

    @pl.when(step == 0)
    def _():
        xp_ref[...] = jnp.zeros(xp_ref.shape, F32)
        xp_ref[pad - hist:pad, :] = c0_ref[0]
        h_ref[...] = h0_ref[0]

    xp_ref[pad:pad + tc, :] = xl_ref[...]
    xc = cb_ref[0]
    for j in range(CONV_W):
        xc = xc + xp_ref[pad - hist + j:pad - hist + j + tc, :] * cw_ref[0, j:j + 1, :]
    new_hist = xp_ref[pad + tc - hist:pad + tc, :]
    xp_ref[pad - hist:pad, :] = new_hist

    xcb = xc.astype(BF)
    ra, ix = [], []
    for n in range(n_blocks):
        sl = slice(n * LRU_BLOCK, (n + 1) * LRU_BLOCK)
        ra.append(_dot(xcb[:, sl], wa_ref[0, n].astype(BF)))
        ix.append(_dot(xcb[:, sl], wx_ref[0, n].astype(BF)))
    r = jax.nn.sigmoid(jnp.concatenate(ra, axis=-1) + ba_ref[0])
    i = jax.nn.sigmoid(jnp.concatenate(ix, axis=-1) + bx_ref[0])
    neg_lam = -lam_ref[0]
    softplus = jnp.maximum(neg_lam, 0.0) + jnp.log1p(jnp.exp(-jnp.abs(neg_lam)))
    log_a = (-LRU_C) * r * softplus
    a = jnp.exp(log_a)
    u = jnp.sqrt(-jnp.tanh(log_a) * (a * a + 1.0)) * (i * xc)

    d = 1
    while d < tc:
        a_prev = _shift_rows(a, d, 1.0)
        u_prev = _shift_rows(u, d, 0.0)
        u = a * u_prev + u
        a = a * a_prev
        d *= 2
    h = a * h_ref[...] + u
    h_ref[...] = h[tc - 1:tc, :]

    gl = gl_ref[...]
    gelu = 0.5 * gl * (1.0 + jnp.tanh(0.7978845608028654 * (gl + 0.044715 * (gl * gl * gl))))
    yl = h * gelu
    ms = jnp.mean(yl * yl, axis=-1, keepdims=True)
    y_ref[...] = yl * lax.rsqrt(ms + EPS) * gol_ref[0]

    @pl.when(step == pl.num_programs(1) - 1)
    def _():
        hlast_ref[0] = h[tc - 1:tc, :]
        cnew_ref[0] = new_hist


def _lru_mixer(proj, h0, conv0, conv_w, conv_b, w_a, b_a, w_x, b_x, lam, g_ol, layer, *, batch, seq, d_att, d_lru):
    assert d_att == d_lru
    m = proj.shape[0]
    tc = _tile(seq, T_LRU)
    per_b = seq // tc
    n_blocks = d_lru // LRU_BLOCK
    depth = conv_b.shape[0]
    vec = lambda a: a.reshape(depth, 1, d_lru)
    vec_spec = pl.BlockSpec((1, 1, d_lru), lambda b, s: (layer, 0, 0))
    gate_w = pl.BlockSpec((1, n_blocks, LRU_BLOCK, LRU_BLOCK), lambda b, s: (layer, 0, 0, 0))
    state = pl.BlockSpec((1, 1, d_lru), lambda b, s: (b, 0, 0))
    hist = pl.BlockSpec((1, CONV_W - 1, d_lru), lambda b, s: (b, 0, 0))
    return pl.pallas_call(
        functools.partial(_lru_body, tc=tc, n_blocks=n_blocks),
        grid=(batch, per_b),
        in_specs=[pl.BlockSpec((tc, d_lru), lambda b, s: (b * per_b + s, 3)),
                  pl.BlockSpec((tc, d_lru), lambda b, s: (b * per_b + s, 4)),
                  state, hist,
                  pl.BlockSpec((1, CONV_W, d_lru), lambda b, s: (layer, 0, 0)), vec_spec,
                  gate_w, vec_spec, gate_w, vec_spec, vec_spec, vec_spec],
        out_specs=[pl.BlockSpec((tc, d_lru), lambda b, s: (b * per_b + s, 0)), state, hist],
        out_shape=[jax.ShapeDtypeStruct((m, d_lru), F32),
                   jax.ShapeDtypeStruct((batch, 1, d_lru), F32),
                   jax.ShapeDtypeStruct((batch, CONV_W - 1, d_lru), F32)],
        scratch_shapes=[pltpu.VMEM((tc + SUBLANES, d_lru), F32), pltpu.VMEM((1, d_lru), F32)],
        compiler_params=_params(2),
        name="lru_mixer",
    )(proj, proj, h0.reshape(batch, 1, d_lru), conv0, conv_w, vec(conv_b), w_a, vec(b_a), w_x, vec(b_x),
      vec(lam), vec(g_ol))


def _out_proj_body(o_ref, yl_ref, goa_ref, w_ref, x_ref, gt_ref, xo_ref, wbf_ref, *, d_att):
    @pl.when(pl.program_id(1) == 0)
    def _():
        wbf_ref[...] = w_ref[0].astype(BF)

    o = o_ref[...]
    ms = jnp.mean(o * o, axis=-1, keepdims=True)
    ya = (o * lax.rsqrt(ms + EPS) * goa_ref[0]).astype(BF)
    y = _dot(ya, wbf_ref[0:d_att, :]) + _dot(yl_ref[...].astype(BF), wbf_ref[d_att:, :])
    xo_ref[...] = x_ref[...] + gt_ref[0] * y


def _out_proj(o_att, y_lru, g_oa, w_out, x, mod_g, layer, *, gate_chunk):
    m, d = x.shape
    d_att, d_lru = o_att.shape[1], y_lru.shape[1]
    groups, r, _ = mod_g.shape
    rows_per_group = m // groups
    tm = _tile(rows_per_group, TM_MATMUL // 2)
    assert r in (1, tm)
    per_group = rows_per_group // tm
    tn = _tile(d, TN_MATMUL)
    n_col = d // tn
    return pl.pallas_call(
        functools.partial(_out_proj_body, d_att=d_att),
        grid=(n_col, m // tm),
        in_specs=[pl.BlockSpec((tm, d_att), lambda j, i: (i, 0)),
                  pl.BlockSpec((tm, d_lru), lambda j, i: (i, 0)),
                  pl.BlockSpec((1, 1, d_att), lambda j, i: (layer, 0, 0)),
                  pl.BlockSpec((1, d, tn), lambda j, i: (layer, 0, j)),
                  pl.BlockSpec((tm, tn), lambda j, i: (i, j)),
                  pl.BlockSpec((1, r, tn), lambda j, i: (i // per_group, 0, gate_chunk * n_col + j))],
        out_specs=pl.BlockSpec((tm, tn), lambda j, i: (i, j)),
        out_shape=jax.ShapeDtypeStruct((m, d), F32),
        scratch_shapes=[pltpu.VMEM((d, tn), BF)],
        compiler_params=_params(2),
        name="out_proj",
    )(o_att, y_lru, g_oa.reshape(-1, 1, d_att), w_out, x, mod_g)


def _swiglu_partial(x, w1_ref, w3_ref, w2_ref):
    h1 = _dot(x, w1_ref.astype(BF))
    h3 = _dot(x, w3_ref.astype(BF))
    hidden = (h1 * jax.nn.sigmoid(h1) * h3).astype(BF)
    return _dot(hidden, w2_ref.astype(BF))


def _ffn_body(x_ref, w1_ref, w3_ref, w2_ref, o_ref):
    @pl.when(pl.program_id(1) == 0)
    def _():
        o_ref[...] = jnp.zeros(o_ref.shape, F32)

    o_ref[...] += _swiglu_partial(x_ref[...], w1_ref[0], w3_ref[0], w2_ref[0])


def _dense_ffn(xn, w1, w3, w2, idx):
    m, d = xn.shape
    d_ff = w1.shape[2]
    tm, tf = _tile(m, TM_FFN), _tile(d_ff, TF_FFN)
    return pl.pallas_call(
        _ffn_body,
        grid=(m // tm, d_ff // tf),
        in_specs=[pl.BlockSpec((tm, d), lambda i, f: (i, 0)),
                  pl.BlockSpec((1, d, tf), lambda i, f: (idx, 0, f)),
                  pl.BlockSpec((1, d, tf), lambda i, f: (idx, 0, f)),
                  pl.BlockSpec((1, tf, d), lambda i, f: (idx, f, 0))],
        out_specs=pl.BlockSpec((tm, d), lambda i, f: (i, 0)),
        out_shape=jax.ShapeDtypeStruct((m, d), F32),
        compiler_params=_params(2),
        name="dense_ffn",
    )(xn, w1, w3, w2)


def _router_body(x_ref, w_ref, idx_ref, p_ref, cnt_out_ref, cnt_ref, *, n_experts):
    @pl.when(pl.program_id(0) == 0)
    def _():
        cnt_ref[...] = jnp.zeros(cnt_ref.shape, F32)

    logits = _dot(x_ref[...].astype(BF), w_ref[0].astype(BF))
    lane = lax.broadcasted_iota(jnp.int32, logits.shape, 1)
    logits = jnp.where(lane < n_experts, logits, NEG_INF)
    top = []
    g = logits
    lane_f = lane.astype(F32)
    for _ in range(TOP_K):
        mx = jnp.max(g, axis=-1, keepdims=True)
        idx = jnp.min(jnp.where(g == mx, lane_f, float(LANES)), axis=-1, keepdims=True)
        g = jnp.where(lane_f == idx, NEG_INF, g)
        top.append((mx, idx))
    (v0, i0), (v1, i1) = top
    e1 = jnp.exp(v1 - v0)
    p0 = 1.0 / (1.0 + e1)
    p1 = e1 / (1.0 + e1)

    tm = logits.shape[0]
    tri = lax.broadcasted_iota(jnp.int32, (tm, tm), 0) >= lax.broadcasted_iota(jnp.int32, (tm, tm), 1)
    tri = jnp.where(tri, 1.0, 0.0).astype(BF)
    ranks, count = [], cnt_ref[...]
    for ik in (i0, i1):
        hot = jnp.where(lane_f == ik, 1.0, 0.0)
        prefix = _dot(tri, hot.astype(BF))
        ranks.append(jnp.sum((count + prefix - 1.0) * hot, axis=-1, keepdims=True))
        count = count + jnp.sum(hot, axis=0, keepdims=True)
    cnt_ref[...] = count
    cnt_out_ref[...] = count.astype(jnp.int32)
    out = jnp.where(lane == 0, i0, jnp.where(lane == 1, i1, jnp.where(lane == 2, ranks[0],
                                                                       jnp.where(lane == 3, ranks[1], 0.0))))
    idx_ref[...] = out.astype(jnp.int32)
    p_ref[...] = jnp.where(lane == 0, p0, jnp.where(lane == 1, p1, 0.0))


def _router(xn, router_w_pad, idx, n_experts):
    m, d = xn.shape
    tm = max(t for t in range(SUBLANES, TM_NORM + 1, SUBLANES) if m % t == 0)
    row = pl.BlockSpec((tm, LANES), lambda i: (i, 0))
    return pl.pallas_call(
        functools.partial(_router_body, n_experts=n_experts),
        grid=(m // tm,),
        in_specs=[pl.BlockSpec((tm, d), lambda i: (i, 0)), pl.BlockSpec((1, d, LANES), lambda i: (idx, 0, 0))],
        out_specs=[row, row, pl.BlockSpec((1, LANES), lambda i: (0, 0))],
        out_shape=[jax.ShapeDtypeStruct((m, LANES), jnp.int32), jax.ShapeDtypeStruct((m, LANES), F32),
                   jax.ShapeDtypeStruct((1, LANES), jnp.int32)],
        scratch_shapes=[pltpu.VMEM((1, LANES), F32)],
        compiler_params=_params(1),
        name="router",
    )(xn, router_w_pad)


def _expert_body(te_ref, used_ref, valid_ref, tok_ref, x_hbm, w1_ref, w3_ref, w2_ref, o_ref, xs_ref, xbf_ref, sem,
                 *, tm):
    t, f = pl.program_id(0), pl.program_id(1)
    half = tm // 2
    active = t < used_ref[0]
    small = jnp.logical_and(active, valid_ref[t] <= half)
    big = jnp.logical_and(active, valid_ref[t] > half)

    def gather(n_rows):
        def issue(r, c):
            tok = tok_ref[t * tm + r]
            pltpu.make_async_copy(x_hbm.at[pl.ds(tok, 1), :], xs_ref.at[pl.ds(r, 1), :], sem.at[0]).start()
            return c

        lax.fori_loop(0, n_rows, issue, 0)
        pltpu.make_async_copy(x_hbm.at[pl.ds(0, n_rows), :], xs_ref.at[pl.ds(0, n_rows), :], sem.at[0]).wait()
        xbf_ref[0:n_rows, :] = xs_ref[0:n_rows, :].astype(BF)

    @pl.when(jnp.logical_and(big, f == 0))
    def _():
        gather(tm)

    @pl.when(jnp.logical_and(small, f == 0))
    def _():
        gather(half)

    @pl.when(f == 0)
    def _():
        o_ref[...] = jnp.zeros(o_ref.shape, F32)

    @pl.when(big)
    def _():
        o_ref[...] += _swiglu_partial(xbf_ref[...], w1_ref[0, 0], w3_ref[0, 0], w2_ref[0, 0])

    @pl.when(small)
    def _():
        o_ref[0:half, :] += _swiglu_partial(xbf_ref[0:half, :], w1_ref[0, 0], w3_ref[0, 0], w2_ref[0, 0])


def _expert_ffn(xn, w1, w3, w2, idx, tile_expert, n_used, tile_valid, slot_token, *, tm):
    m, d = xn.shape
    d_ff = w1.shape[3]
    n_tiles = tile_expert.shape[0]
    tf = _tile(d_ff, TF_FFN)
    n_f = d_ff // tf
    assert tm % (2 * SUBLANES) == 0

    def fcol(t, f, used):
        return jnp.where(t < used[0], f, n_f - 1)

    grid_spec = pltpu.PrefetchScalarGridSpec(
        num_scalar_prefetch=4,
        grid=(n_tiles, n_f),
        in_specs=[pl.BlockSpec(memory_space=pl.ANY),
                  pl.BlockSpec((1, 1, d, tf), lambda t, f, te, used, nv, tok: (idx, te[t], 0, fcol(t, f, used))),
                  pl.BlockSpec((1, 1, d, tf), lambda t, f, te, used, nv, tok: (idx, te[t], 0, fcol(t, f, used))),
                  pl.BlockSpec((1, 1, tf, d), lambda t, f, te, used, nv, tok: (idx, te[t], fcol(t, f, used), 0))],
        out_specs=pl.BlockSpec((tm, d), lambda t, f, te, used, nv, tok: (t, 0)),
        scratch_shapes=[pltpu.VMEM((tm, d), F32), pltpu.VMEM((tm, d), BF), pltpu.SemaphoreType.DMA((1,))],
    )
    return pl.pallas_call(
        functools.partial(_expert_body, tm=tm),
        grid_spec=grid_spec,
        out_shape=jax.ShapeDtypeStruct((n_tiles * tm, d), F32),
        compiler_params=_params(2),
        name="expert_ffn",
    )(tile_expert, n_used, tile_valid, slot_token, xn, w1, w3, w2)


def _combine_body(pos_ref, ys_hbm, p_ref, o_ref, a_ref, b_ref, sem, *, tm):
    base = pl.program_id(0) * tm

    def issue(r, c):
        s0 = pos_ref[(base + r) * TOP_K]
        s1 = pos_ref[(base + r) * TOP_K + 1]
        pltpu.make_async_copy(ys_hbm.at[pl.ds(s0, 1), :], a_ref.at[pl.ds(r, 1), :], sem.at[0]).start()
        pltpu.make_async_copy(ys_hbm.at[pl.ds(s1, 1), :], b_ref.at[pl.ds(r, 1), :], sem.at[1]).start()
        return c

    lax.fori_loop(0, tm, issue, 0)
    pltpu.make_async_copy(ys_hbm.at[pl.ds(0, tm), :], a_ref, sem.at[0]).wait()
    pltpu.make_async_copy(ys_hbm.at[pl.ds(0, tm), :], b_ref, sem.at[1]).wait()
    o_ref[...] = p_ref[:, 0:1] * a_ref[...] + p_ref[:, 1:2] * b_ref[...]


def _combine(ys, pos, top_p, row_offset, m):
    d = ys.shape[1]
    tm = _tile(m, TM_COMBINE)
    assert row_offset % tm == 0
    first = row_offset // tm
    grid_spec = pltpu.PrefetchScalarGridSpec(
        num_scalar_prefetch=1,
        grid=(m // tm,),
        in_specs=[pl.BlockSpec(memory_space=pl.ANY), pl.BlockSpec((tm, LANES), lambda i, pos_: (first + i, 0))],
        out_specs=pl.BlockSpec((tm, d), lambda i, pos_: (i, 0)),
        scratch_shapes=[pltpu.VMEM((tm, d), F32), pltpu.VMEM((tm, d), F32), pltpu.SemaphoreType.DMA((2,))],
    )
    return pl.pallas_call(
        functools.partial(_combine_body, tm=tm),
        grid_spec=grid_spec,
        out_shape=jax.ShapeDtypeStruct((m, d), F32),
        compiler_params=_params(1),
        name="moe_combine",
    )(lax.slice(pos, (row_offset * TOP_K,), ((row_offset + m) * TOP_K,)), ys, top_p)


def _moe_ffn(xns, router_w_pad, w1, w3, w2, idx, n_experts):
    xn = jnp.concatenate(xns, axis=0) if len(xns) > 1 else xns[0]
    m = xn.shape[0]
    top_idx, top_p, counts = _router(xn, router_w_pad, idx, n_experts)
    e_flat = top_idx[:, :TOP_K].reshape(-1)
    rank = top_idx[:, TOP_K:2 * TOP_K].reshape(-1)
    counts = counts[0, :n_experts]
    n_assign = m * TOP_K
    tm = min(TM_FFN, m)
    n_tiles = -(-n_assign // tm) + n_experts
    experts = jnp.arange(n_experts, dtype=jnp.int32)
    tiles_e = (counts + tm - 1) // tm
    tile_end = jnp.cumsum(tiles_e)
    tile_start = tile_end - tiles_e
    one_hot = (e_flat[:, None] == experts[None, :]).astype(jnp.int32)
    slot = (jnp.sum(tile_start[None, :] * one_hot, axis=1) * tm + rank).astype(jnp.int32)
    slot_token = jnp.zeros((n_tiles * tm,), jnp.int32).at[slot].set(jnp.arange(n_assign, dtype=jnp.int32) // TOP_K)
    n_used = tile_end[-1]
    t_ids = jnp.minimum(jnp.arange(n_tiles, dtype=jnp.int32), n_used - 1)
    tile_expert = jnp.sum((t_ids[:, None] >= tile_end[None, :]).astype(jnp.int32), axis=1).astype(jnp.int32)
    own = (tile_expert[:, None] == experts[None, :]).astype(jnp.int32)
    left = counts[None, :] - (t_ids[:, None] - tile_start[None, :]) * tm
    tile_valid = jnp.clip(jnp.sum(own * left, axis=1), 0, tm).astype(jnp.int32)
    ys = _expert_ffn(xn, w1, w3, w2, idx, tile_expert, n_used.reshape(1).astype(jnp.int32), tile_valid,
                     slot_token, tm=tm)
    outs, start = [], 0
    for g in xns:
        rows = g.shape[0]
        outs.append(_combine(ys, slot, top_p, start, rows))
        start += rows
    return outs


def _rope_tables(pos):
    half = HEAD_DIM // 2
    inv = ROPE_THETA ** (-(jnp.arange(half, dtype=F32) / half))
    ang = pos.astype(F32)[:, None] * inv[None, :]
    cos, sin = jnp.cos(ang), jnp.sin(ang)
    return jnp.concatenate([cos, cos], axis=-1), jnp.concatenate([-sin, sin], axis=-1)


class _Trunk:
    def __init__(self, x3, mod_rows, pos, h_init, conv_init, paged, depth):
        self.batch, self.seq, self.d = x3.shape
        self.x = x3.reshape(self.batch * self.seq, self.d)
        self.cos, self.sin_signed = _rope_tables(pos)
        self.h_init, self.conv_init, self.paged = h_init, conv_init, paged
        if self.seq % TM_NORM == 0:
            self.mods = [mod_rows[l][:, None, :] for l in range(depth)]
        else:
            self.mods = [jnp.repeat(mod_rows[l], self.seq, axis=0)[None] for l in range(depth)]
        self.f = None
        self.k_rows, self.v_rows, self.h_fin, self.conv_fin = [], [], [], []

    def mix(self, l, w, moe_layer):
        batch, seq, mods = self.batch, self.seq, self.mods
        m = batch * seq
        d_att = w["g_out_att"].shape[1]
        d_lru = w["g_out_lru"].shape[1]
        n_heads = d_att // HEAD_DIM
        cos, sin_signed, paged = self.cos, self.sin_signed, self.paged
        if l == 0:
            x = self.x
            _, xn = _resid_norm(x, g_norm=w["g_norm_mix"], layer=l, norm_mod=mods[l], scale_chunk=1, shift_chunk=0)
        else:
            x, xn = _resid_norm(self.x, f=self.f, gate_mod=mods[l - 1], gate_chunk=5, g_norm=w["g_norm_mix"],
                                layer=l, norm_mod=mods[l], scale_chunk=1, shift_chunk=0)
        proj = _in_proj(xn, w["w_in"], l)
        if paged is None:
            k_new, v_new, q_t, k_hm, v_t, kmean = _qkv_prep(
                proj, w["g_q"], w["g_k"], cos, sin_signed, l, batch=batch, seq=seq, d_att=d_att, head_major=True)
            n_blocks = seq // MOBA_BLOCK
            km = kmean.reshape(batch, n_blocks, n_heads, HEAD_DIM).transpose(0, 2, 1, 3)
            km = jnp.pad(km, ((0, 0), (0, 0), (0, (-n_blocks) % SUBLANES), (0, 0)))
            o_att = _moba_attention(q_t, k_hm, v_t, km)
        else:
            cache_k, cache_v, page_table = paged
            k_new, v_new, q = _qkv_prep(
                proj, w["g_q"], w["g_k"], cos, sin_signed, l, batch=batch, seq=seq, d_att=d_att, head_major=False)
            q3 = q.reshape(batch, seq, d_att)
            sel = _page_select(q3, cache_k, page_table, l, n_heads=n_heads)
            sel_flat = sel[..., :MOBA_TOPK].reshape(-1)
            o_att = _page_attend(sel_flat, page_table, q3, k_new.reshape(batch, seq, d_att),
                                 v_new.reshape(batch, seq, d_att), cache_k, cache_v, l,
                                 n_heads=n_heads).reshape(m, d_att)
        y_lru, h_last, conv_new = _lru_mixer(
            proj, self.h_init[l], self.conv_init[l], w["conv_w"], w["conv_b"], w["w_gate_a"], w["b_gate_a"],
            w["w_gate_x"], w["b_gate_x"], w["lru_lambda"], w["g_out_lru"], l,
            batch=batch, seq=seq, d_att=d_att, d_lru=d_lru)
        self.x = _out_proj(o_att, y_lru, w["g_out_att"], w["w_out"], x, mods[l], l, gate_chunk=2)
        self.k_rows.append(k_new)
        self.v_rows.append(v_new)
        self.h_fin.append(h_last)
        self.conv_fin.append(conv_new)
        _, xn2 = _resid_norm(self.x, g_norm=w["g_norm_ffn"], layer=l, norm_mod=mods[l], scale_chunk=4,
                             shift_chunk=3, xn_dtype=F32 if moe_layer else BF)
        return xn2

    def finish(self, w):
        depth = len(self.k_rows)
        d_att = w["g_out_att"].shape[1]
        d_lru = w["g_out_lru"].shape[1]
        x, _ = _resid_norm(self.x, f=self.f, gate_mod=self.mods[depth - 1], gate_chunk=5)
        shape_kv = (depth, self.batch, self.seq, d_att // HEAD_DIM, HEAD_DIM)
        return (x.reshape(self.batch, self.seq, self.d), jnp.stack(self.k_rows).reshape(shape_kv),
                jnp.stack(self.v_rows).reshape(shape_kv),
                jnp.stack(self.h_fin).reshape(depth, self.batch, d_lru), jnp.stack(self.conv_fin))


def kernel(x_prompt, x_sample, cache_k, cache_v, state_lru_h, state_lru_conv, page_table, c_prompt, c_sample,
           g_norm_mix, g_norm_ffn, w_mod, b_mod, w_in, g_q, g_k, conv_w, conv_b, w_gate_a, b_gate_a, w_gate_x,
           b_gate_x, lru_lambda, g_out_att, g_out_lru, w_out, ffn_w1, ffn_w3, ffn_w2, router_w, moe_w1, moe_w3,
           moe_w2):
    depth, d = g_norm_mix.shape
    w = dict(g_norm_mix=g_norm_mix.reshape(depth, 1, d), g_norm_ffn=g_norm_ffn.reshape(depth, 1, d), w_in=w_in,
             g_q=g_q, g_k=g_k, conv_w=conv_w, conv_b=conv_b, w_gate_a=w_gate_a, b_gate_a=b_gate_a,
             w_gate_x=w_gate_x, b_gate_x=b_gate_x, lru_lambda=lru_lambda, g_out_att=g_out_att,
             g_out_lru=g_out_lru, w_out=w_out, ffn_w1=ffn_w1, ffn_w3=ffn_w3, ffn_w2=ffn_w2, router_w=router_w,
             moe_w1=moe_w1, moe_w3=moe_w3, moe_w2=moe_w2)
    b_p, s_p, _ = x_prompt.shape
    b_s, s_s, _ = x_sample.shape
    d_att = g_out_att.shape[1]
    d_lru = g_out_lru.shape[1]

    n_cond = b_p + b_s
    pad_rows = (-n_cond) % (2 * SUBLANES)
    c_all = jnp.concatenate([c_prompt, c_sample, jnp.zeros((pad_rows, d), F32)], axis=0)
    mod_all = _modulation(c_all, w_mod, b_mod)

    past_len = page_table.shape[1] * PAGE_SIZE
    pos_s = past_len + jnp.arange(s_s, dtype=jnp.int32)
    sample = _Trunk(x_sample, mod_all[:, b_p:n_cond], pos_s, state_lru_h, state_lru_conv,
                    (cache_k, cache_v, page_table), depth)
    pos_p = jnp.arange(s_p, dtype=jnp.int32)
    prompt = _Trunk(x_prompt, mod_all[:, :b_p], pos_p, jnp.zeros((depth, b_p, d_lru), F32),
                    jnp.zeros((depth, b_p, CONV_W - 1, d_lru), F32), None, depth)
    trunks = (sample, prompt)

    n_experts = router_w.shape[2]
    router_w_pad = jnp.pad(router_w, ((0, 0), (0, 0), (0, LANES - n_experts)))
    for l in range(depth):
        moe_layer = l % 2 == 1
        xn2 = [t.mix(l, w, moe_layer) for t in trunks]
        if moe_layer:
            fs = _moe_ffn(xn2[::-1], router_w_pad, moe_w1, moe_w3, moe_w2, l // 2, n_experts)[::-1]
        else:
            fs = [_dense_ffn(xn, ffn_w1, ffn_w3, ffn_w2, l // 2) for xn in xn2]
        for t, f in zip(trunks, fs):
            t.f = f
    y_s, k_s, v_s, h_s, conv_s = sample.finish(w)
    y_p, k_p, v_p, h_p, conv_p = prompt.finish(w)
    return (y_p, y_s, k_p, v_p, h_p, conv_p, k_s, v_s, h_s, conv_s)
```

```python
import functools

import jax
import jax.numpy as jnp
from jax import lax
from jax.experimental import pallas as pl
from jax.experimental.pallas import tpu as pltpu

BF = jnp.bfloat16
F32 = jnp.float32

HEAD_DIM = 128
PAGE_SIZE = 128
MOBA_BLOCK = 256
MOBA_TOPK = 3
PAGES_PER_BLOCK = MOBA_BLOCK // PAGE_SIZE
CONV_W = 4
LRU_C = 8.0
LRU_BLOCK = 128
TOP_K = 2
ROPE_THETA = 10000.0
EPS = 1e-6
LANES = 128
SUBLANES = 8
NEG_INF = float("-inf")

TM_MATMUL = 1024
TN_MATMUL = 1024
TM_NORM = 256
TM_FFN = 1024
TF_FFN = 256
T_LRU = 256
TM_COMBINE = 256
TN_MOD = 1024
PAGES_PER_STEP = 8
HEADS_PER_STEP = 8
VMEM_LIMIT = 56 << 20


def _params(n_axes, vmem=VMEM_LIMIT):
    return pltpu.CompilerParams(dimension_semantics=("arbitrary",) * n_axes, vmem_limit_bytes=vmem)


def _tile(n, pref):
    t = min(n, pref)
    assert n % t == 0, (n, pref)
    return t


def _nt_dot(a, b):
    return lax.dot_general(a, b, (((1,), (1,)), ((), ())), preferred_element_type=F32)


def _dot(a, b):
    return jnp.dot(a, b, preferred_element_type=F32)


def _mod_body(c_ref, w_ref, b_ref, o_ref):
    c = c_ref[...]
    cond = (c * jax.nn.sigmoid(c)).astype(BF)
    o_ref[0] = _dot(cond, w_ref[0].astype(BF)) + b_ref[0]


def _modulation(c_all, w_mod, b_mod):
    depth, d, n = w_mod.shape
    rows = c_all.shape[0]
    tn = _tile(n, TN_MOD)
    return pl.pallas_call(
        _mod_body,
        grid=(depth, n // tn),
        in_specs=[pl.BlockSpec((rows, d), lambda l, j: (0, 0)),
                  pl.BlockSpec((1, d, tn), lambda l, j: (l, 0, j)),
                  pl.BlockSpec((1, 1, tn), lambda l, j: (l, 0, j))],
        out_specs=pl.BlockSpec((1, rows, tn), lambda l, j: (l, 0, j)),
        out_shape=jax.ShapeDtypeStruct((depth, rows, n), F32),
        compiler_params=_params(2),
        name="modulation",
    )(c_all, w_mod, b_mod.reshape(depth, 1, n))


def _resid_norm_body(*refs, has_res, do_norm):
    refs = list(refs)
    x_ref = refs.pop(0)
    x = x_ref[...]
    if has_res:
        f_ref, gt_ref = refs.pop(0), refs.pop(0)
        x = x + gt_ref[0] * f_ref[...]
    if do_norm:
        g_ref, sc_ref, sh_ref = refs.pop(0), refs.pop(0), refs.pop(0)
    if has_res:
        refs.pop(0)[...] = x
    if do_norm:
        ms = jnp.mean(x * x, axis=-1, keepdims=True)
        y = x * lax.rsqrt(ms + EPS) * g_ref[0]
        xn_ref = refs.pop(0)
        xn_ref[...] = (y * (1.0 + sc_ref[0]) + sh_ref[0]).astype(xn_ref.dtype)


def _resid_norm(x, *, f=None, gate_mod=None, gate_chunk=None, g_norm=None, layer=None, norm_mod=None,
                scale_chunk=None, shift_chunk=None, xn_dtype=None):
    m, d = x.shape
    has_res, do_norm = f is not None, g_norm is not None
    groups, r, _ = (gate_mod if has_res else norm_mod).shape
    rows_per_group = m // groups
    tm = _tile(rows_per_group, TM_NORM)
    assert r in (1, tm)
    per_group = rows_per_group // tm

    def mod_spec(chunk):
        return pl.BlockSpec((1, r, d), lambda i: (i // per_group, 0, chunk))

    row_spec = pl.BlockSpec((tm, d), lambda i: (i, 0))
    args, in_specs, out_specs, out_shape = [x], [row_spec], [], []
    if has_res:
        args += [f, gate_mod]
        in_specs += [row_spec, mod_spec(gate_chunk)]
        out_specs.append(row_spec)
        out_shape.append(jax.ShapeDtypeStruct((m, d), F32))
    if do_norm:
        assert norm_mod.shape[:2] == (groups, r)
        args += [g_norm, norm_mod, norm_mod]
        in_specs += [pl.BlockSpec((1, 1, d), lambda i: (layer, 0, 0)), mod_spec(scale_chunk), mod_spec(shift_chunk)]
        out_specs.append(row_spec)
        out_shape.append(jax.ShapeDtypeStruct((m, d), xn_dtype or BF))
    outs = pl.pallas_call(
        functools.partial(_resid_norm_body, has_res=has_res, do_norm=do_norm),
        grid=(m // tm,),
        in_specs=in_specs, out_specs=out_specs, out_shape=out_shape,
        compiler_params=_params(1),
        name="resid_norm",
    )(*args)
    x_new = outs[0] if has_res else x
    xn = outs[-1] if do_norm else None
    return x_new, xn


def _in_proj_body(a_ref, w_ref, o_ref, wbf_ref):
    @pl.when(pl.program_id(1) == 0)
    def _():
        wbf_ref[...] = w_ref[0].astype(BF)

    o_ref[...] = _dot(a_ref[...], wbf_ref[...])


def _in_proj(xn, w_in, layer):
    m, d = xn.shape
    n = w_in.shape[2]
    tm, tn = _tile(m, TM_MATMUL), _tile(n, TN_MATMUL)
    return pl.pallas_call(
        _in_proj_body,
        grid=(n // tn, m // tm),
        in_specs=[pl.BlockSpec((tm, d), lambda j, i: (i, 0)),
                  pl.BlockSpec((1, d, tn), lambda j, i: (layer, 0, j))],
        out_specs=pl.BlockSpec((tm, tn), lambda j, i: (i, j)),
        out_shape=jax.ShapeDtypeStruct((m, n), F32),
        scratch_shapes=[pltpu.VMEM((d, tn), BF)],
        compiler_params=_params(2),
        name="in_proj",
    )(xn, w_in)


def _head_norm_rope(x, g, cos, sin_signed):
    ms = jnp.mean(x * x, axis=-1, keepdims=True)
    y = x * lax.rsqrt(ms + EPS) * g
    return y * cos + pltpu.roll(y, HEAD_DIM // 2, axis=1) * sin_signed


def _prep_body(q_ref, k_ref, v_ref, gq_ref, gk_ref, cos_ref, sin_ref, *out_refs, n_heads, q_scale, head_major):
    cos, sin_signed = cos_ref[...], sin_ref[...]
    gq, gk = gq_ref[0], gk_ref[0]
    if head_major:
        kleaf_ref, vleaf_ref, qt_ref, khm_ref, vt_ref, kmean_ref = out_refs
    else:
        kleaf_ref, vleaf_ref, q_out_ref = out_refs
    v = v_ref[...]
    vleaf_ref[...] = v
    for h0 in range(0, n_heads, 2):
        done = []
        for h in range(h0, min(h0 + 2, n_heads)):
            sl = slice(h * HEAD_DIM, (h + 1) * HEAD_DIM)
            qh = _head_norm_rope(q_ref[:, sl], gq, cos, sin_signed) * q_scale
            kh = _head_norm_rope(k_ref[:, sl], gk, cos, sin_signed)
            if head_major:
                done.append((h, sl, kh, qh.T.astype(BF), kh.astype(BF), v[:, sl].T.astype(BF),
                             jnp.mean(kh, axis=0, keepdims=True)))
            else:
                done.append((h, sl, kh, qh))
        for item in done:
            h, sl, kh = item[:3]
            kleaf_ref[:, sl] = kh
            if head_major:
                qt_ref[0, h], khm_ref[0, h], vt_ref[0, h], kmean_ref[0, :, sl] = item[3:]
            else:
                q_out_ref[:, sl] = item[3]


def _qkv_prep(proj, g_q, g_k, cos, sin_signed, layer, *, batch, seq, d_att, head_major):
    m = proj.shape[0]
    n_heads = d_att // HEAD_DIM
    ts = MOBA_BLOCK if head_major else seq
    assert seq % ts == 0
    per_b = seq // ts
    col = lambda c: pl.BlockSpec((ts, d_att), lambda i: (i, c))
    gain = pl.BlockSpec((1, 1, HEAD_DIM), lambda i: (layer, 0, 0))
    table = pl.BlockSpec((ts, HEAD_DIM), lambda i: (i % per_b, 0))
    leaf = pl.BlockSpec((ts, d_att), lambda i: (i, 0))
    leaf_shape = jax.ShapeDtypeStruct((m, d_att), F32)
    if head_major:
        hm = pl.BlockSpec((1, n_heads, ts, HEAD_DIM), lambda i: (i // per_b, 0, i % per_b, 0))
        hm_shape = jax.ShapeDtypeStruct((batch, n_heads, seq, HEAD_DIM), BF)
        hmt = pl.BlockSpec((1, n_heads, HEAD_DIM, ts), lambda i: (i // per_b, 0, 0, i % per_b))
        hmt_shape = jax.ShapeDtypeStruct((batch, n_heads, HEAD_DIM, seq), BF)
        out_specs = [leaf, leaf, hmt, hm, hmt, pl.BlockSpec((1, 1, d_att), lambda i: (i, 0, 0))]
        out_shape = [leaf_shape, leaf_shape, hmt_shape, hm_shape, hmt_shape,
                     jax.ShapeDtypeStruct((m // ts, 1, d_att), F32)]
    else:
        out_specs = [leaf, leaf, leaf]
        out_shape = [leaf_shape, leaf_shape, leaf_shape]
    return pl.pallas_call(
        functools.partial(_prep_body, n_heads=n_heads, q_scale=HEAD_DIM ** -0.5, head_major=head_major),
        grid=(m // ts,),
        in_specs=[col(0), col(1), col(2), gain, gain, table, table],
        out_specs=out_specs, out_shape=out_shape,
        compiler_params=_params(1),
        name="qkv_prep",
    )(proj, proj, proj, g_q.reshape(-1, 1, HEAD_DIM), g_k.reshape(-1, 1, HEAD_DIM), cos, sin_signed)


def _top_k_lane_mask(g, k):
    lane = lax.broadcasted_iota(jnp.int32, g.shape, 1).astype(F32)
    sel = jnp.zeros(g.shape, F32)
    picks = []
    for _ in range(k):
        m = jnp.max(g, axis=-1, keepdims=True)
        hit = jnp.logical_and(g == m, m > NEG_INF)
        idx = jnp.min(jnp.where(hit, lane, float(LANES)), axis=-1, keepdims=True)
        pick = lane == idx
        sel = jnp.where(pick, 1.0, sel)
        g = jnp.where(pick, NEG_INF, g)
        picks.append(idx)
    return sel, picks


def _top_k_row_mask(g, k):
    blk = lax.broadcasted_iota(jnp.int32, g.shape, 0).astype(F32)
    sel = jnp.zeros(g.shape, F32)
    for _ in range(k):
        m = jnp.max(g, axis=0, keepdims=True)
        hit = jnp.logical_and(g == m, m > NEG_INF)
        idx = jnp.min(jnp.where(hit, blk, float(g.shape[0])), axis=0, keepdims=True)
        pick = blk == idx
        sel = jnp.where(pick, 1.0, sel)
        g = jnp.where(pick, NEG_INF, g)
    return sel


def _moba_body(qt_ref, k_ref, vt_ref, kd_ref, vtd_ref, km_ref, o_ref, m_ref, l_ref, acc_ref, sel_ref, *, n_blocks):
    qi = pl.program_id(2)
    n_group = qt_ref.shape[1]

    first = []
    for g in range(n_group):
        qt = qt_ref[0, g]
        gate = _dot(km_ref[0, g].astype(BF), qt)
        blk = lax.broadcasted_iota(jnp.int32, gate.shape, 0)
        sel = _top_k_row_mask(jnp.where(blk < qi, gate, NEG_INF), MOBA_TOPK)

        s = _dot(kd_ref[0, g], qt)
        key = lax.broadcasted_iota(jnp.int32, s.shape, 0)
        qry = lax.broadcasted_iota(jnp.int32, s.shape, 1)
        s = jnp.where(key <= qry, s, NEG_INF)
        m0 = jnp.max(s, axis=0, keepdims=True)
        p = jnp.exp(s - m0)
        first.append((sel, m0, jnp.sum(p, axis=0, keepdims=True), _dot(vtd_ref[0, g], p.astype(BF))))
    for g, (sel, m0, l0, acc0) in enumerate(first):
        sel_ref[g] = sel
        m_ref[g] = m0
        l_ref[g] = l0
        acc_ref[g] = acc0

    for j in range(n_blocks - 1):
        @pl.when(j < qi)
        def _(j=j):
            span = slice(j * MOBA_BLOCK, (j + 1) * MOBA_BLOCK)
            heads = range(n_group)
            sj = [_dot(k_ref[0, g, span, :], qt_ref[0, g]) for g in heads]
            keep = [sel_ref[g, j:j + 1, :] > 0.5 for g in heads]
            m_old = [m_ref[g] for g in heads]
            l_old = [l_ref[g] for g in heads]
            acc_old = [acc_ref[g] for g in heads]
            m_new, l_new, acc_new = [], [], []
            for g in heads:
                s_g = jnp.where(keep[g], sj[g], NEG_INF)
                m_g = jnp.maximum(m_old[g], jnp.max(s_g, axis=0, keepdims=True))
                alpha = jnp.exp(m_old[g] - m_g)
                p_g = jnp.exp(s_g - m_g)
                m_new.append(m_g)
                l_new.append(alpha * l_old[g] + jnp.sum(p_g, axis=0, keepdims=True))
                acc_new.append(alpha * acc_old[g] + _dot(vt_ref[0, g, :, span], p_g.astype(BF)))
            for g in heads:
                m_ref[g] = m_new[g]
                l_ref[g] = l_new[g]
                acc_ref[g] = acc_new[g]

    outs = [(acc_ref[g] / l_ref[g]).T for g in range(n_group)]
    for g in range(n_group):
        o_ref[:, g * HEAD_DIM:(g + 1) * HEAD_DIM] = outs[g]


def _moba_attention(q_t, k_hm, v_t, kmean):
    batch, n_heads, seq, _ = k_hm.shape
    n_blocks = seq // MOBA_BLOCK
    nb_pad = kmean.shape[2]
    hg = _tile(n_heads, HEADS_PER_STEP)
    rows = pl.BlockSpec((1, hg, MOBA_BLOCK, HEAD_DIM), lambda b, h, i: (b, h, i, 0))
    cols = pl.BlockSpec((1, hg, HEAD_DIM, MOBA_BLOCK), lambda b, h, i: (b, h, 0, i))
    all_rows = pl.BlockSpec((1, hg, seq, HEAD_DIM), lambda b, h, i: (b, h, 0, 0))
    all_cols = pl.BlockSpec((1, hg, HEAD_DIM, seq), lambda b, h, i: (b, h, 0, 0))
    return pl.pallas_call(
        functools.partial(_moba_body, n_blocks=n_blocks),
        grid=(batch, n_heads // hg, n_blocks),
        in_specs=[cols, all_rows, all_cols, rows, cols,
                  pl.BlockSpec((1, hg, nb_pad, HEAD_DIM), lambda b, h, i: (b, h, 0, 0))],
        out_specs=pl.BlockSpec((MOBA_BLOCK, hg * HEAD_DIM), lambda b, h, i: (b * n_blocks + i, h)),
        out_shape=jax.ShapeDtypeStruct((batch * seq, n_heads * HEAD_DIM), F32),
        scratch_shapes=[pltpu.VMEM((hg, 1, MOBA_BLOCK), F32), pltpu.VMEM((hg, 1, MOBA_BLOCK), F32),
                        pltpu.VMEM((hg, HEAD_DIM, MOBA_BLOCK), F32), pltpu.VMEM((hg, nb_pad, MOBA_BLOCK), F32)],
        compiler_params=_params(3),
        name="moba_attention",
    )(q_t, k_hm, v_t, k_hm, v_t, kmean)


def _page_select_body(pt_ref, q_ref, *refs, n_heads, n_past_blocks, pages_per_step):
    page_refs, (sel_ref, km_ref) = refs[:pages_per_step], refs[pages_per_step:]
    g = pl.program_id(1)

    @pl.when(g == 0)
    def _():
        km_ref[...] = jnp.zeros(km_ref.shape, F32)

    blocks_per_step = pages_per_step // PAGES_PER_BLOCK
    for r in range(blocks_per_step):
        tot = jnp.zeros((n_heads, HEAD_DIM), F32)
        for half in range(PAGES_PER_BLOCK):
            tot = tot + jnp.sum(page_refs[r * PAGES_PER_BLOCK + half][0, 0], axis=0)
        tot = tot * (1.0 / MOBA_BLOCK)
        for h in range(n_heads):
            km_ref[h, pl.ds(g * blocks_per_step + r, 1), :] = tot[h:h + 1, :]

    @pl.when(g == pl.num_programs(1) - 1)
    def _():
        for h in range(n_heads):
            sl = slice(h * HEAD_DIM, (h + 1) * HEAD_DIM)
            gate = _nt_dot(q_ref[0, :, sl].astype(BF), km_ref[h].astype(BF))
            lane = lax.broadcasted_iota(jnp.int32, gate.shape, 1)
            gate = jnp.where(lane < n_past_blocks, gate, NEG_INF)
            _, picks = _top_k_lane_mask(gate, MOBA_TOPK)
            out = jnp.full(gate.shape, -1.0, F32)
            for t, idx in enumerate(picks):
                out = jnp.where(lane == t, jnp.where(idx < float(LANES), idx, -1.0), out)
            sel_ref[0, h] = out.astype(jnp.int32)


def _page_select(q, cache_k, page_table, layer, *, n_heads):
    batch, dec_seq, d_att = q.shape
    n_pages = page_table.shape[1]
    n_past_blocks = n_pages // PAGES_PER_BLOCK
    assert n_pages % PAGES_PER_BLOCK == 0 and n_past_blocks <= LANES
    pps = _tile(n_pages, PAGES_PER_STEP)
    assert pps % PAGES_PER_BLOCK == 0

    def page_spec(r):
        return pl.BlockSpec((1, 1, PAGE_SIZE, n_heads, HEAD_DIM),
                            lambda b, g, pt: (layer, pt[b * n_pages + g * pps + r], 0, 0, 0))

    grid_spec = pltpu.PrefetchScalarGridSpec(
        num_scalar_prefetch=1,
        grid=(batch, n_pages // pps),
        in_specs=[pl.BlockSpec((1, dec_seq, d_att), lambda b, g, pt: (b, 0, 0))] + [page_spec(r) for r in range(pps)],
        out_specs=pl.BlockSpec((1, n_heads, dec_seq, LANES), lambda b, g, pt: (b, 0, 0, 0)),
        scratch_shapes=[pltpu.VMEM((n_heads, LANES, HEAD_DIM), F32)],
    )
    return pl.pallas_call(
        functools.partial(_page_select_body, n_heads=n_heads, n_past_blocks=n_past_blocks, pages_per_step=pps),
        grid_spec=grid_spec,
        out_shape=jax.ShapeDtypeStruct((batch, n_heads, dec_seq, LANES), jnp.int32),
        compiler_params=_params(2),
        name="page_select",
    )(page_table.reshape(-1), q, *([cache_k] * pps))


def _page_attend_body(sel_ref, pt_ref, q_ref, kn_ref, vn_ref, ck_ref, cv_ref, o_ref, kbuf, vbuf, sem,
                      *, layer, n_heads, n_pages, dec_seq):
    step = pl.program_id(0)
    n_pairs = dec_seq * MOBA_TOPK
    rows_per_query = MOBA_TOPK * MOBA_BLOCK

    def copies(s, slot):
        b, h = s // n_heads, s % n_heads
        out = []
        for n in range(n_pairs):
            blk = jnp.maximum(sel_ref[s * n_pairs + n], 0)
            for half in range(PAGES_PER_BLOCK):
                page = pt_ref[b * n_pages + blk * PAGES_PER_BLOCK + half]
                rows = pl.ds(n * MOBA_BLOCK + half * PAGE_SIZE, PAGE_SIZE)
                out.append(pltpu.make_async_copy(ck_ref.at[layer, page, :, h, :], kbuf.at[slot, rows, :],
                                                 sem.at[0, slot]))
                out.append(pltpu.make_async_copy(cv_ref.at[layer, page, :, h, :], vbuf.at[slot, rows, :],
                                                 sem.at[1, slot]))
        return out

    @pl.when(step == 0)
    def _():
        for c in copies(0, 0):
            c.start()

    @pl.when(step + 1 < pl.num_programs(0))
    def _():
        for c in copies(step + 1, (step + 1) % 2):
            c.start()

    slot = step % 2
    for c in copies(step, slot):
        c.wait()

    q_pad = jnp.concatenate([q_ref[0], jnp.zeros((LANES - dec_seq, HEAD_DIM), F32)], axis=0).astype(BF)
    s_past = _nt_dot(kbuf[slot].astype(BF), q_pad)
    key = lax.broadcasted_iota(jnp.int32, s_past.shape, 0)
    qry = lax.broadcasted_iota(jnp.int32, s_past.shape, 1)
    first = qry * rows_per_query
    s_past = jnp.where(jnp.logical_and(key >= first, key < first + rows_per_query), s_past, NEG_INF)
    s_own = _nt_dot(kn_ref[0].astype(BF), q_pad)
    key_o = lax.broadcasted_iota(jnp.int32, s_own.shape, 0)
    qry_o = lax.broadcasted_iota(jnp.int32, s_own.shape, 1)
    s_own = jnp.where(key_o <= qry_o, s_own, NEG_INF)
    m = jnp.maximum(jnp.max(s_past, axis=0, keepdims=True), jnp.max(s_own, axis=0, keepdims=True))
    p_past = jnp.exp(s_past - m)
    p_own = jnp.exp(s_own - m)
    inv_l = 1.0 / (jnp.sum(p_past, axis=0, keepdims=True) + jnp.sum(p_own, axis=0, keepdims=True))

    v_past, v_own = vbuf[slot], vn_ref[0]
    out_rows = []
    for i in range(dec_seq):
        span = slice(i * rows_per_query, (i + 1) * rows_per_query)
        acc = jnp.sum(p_past[span, i:i + 1] * v_past[span, :], axis=0, keepdims=True)
        acc = acc + jnp.sum(p_own[:, i:i + 1] * v_own, axis=0, keepdims=True)
        out_rows.append(acc * inv_l[:, i:i + 1])
    o_ref[0] = jnp.concatenate(out_rows, axis=0)


def _page_attend(sel, page_table, q, k_new, v_new, cache_k, cache_v, layer, *, n_heads):
    batch, dec_seq, d_att = q.shape
    n_pages = page_table.shape[1]
    assert n_pages // PAGES_PER_BLOCK >= MOBA_TOPK and dec_seq <= SUBLANES
    n_rows = dec_seq * MOBA_TOPK * MOBA_BLOCK
    head = pl.BlockSpec((1, dec_seq, HEAD_DIM), lambda s, s_, p_: (s // n_heads, 0, s % n_heads))
    grid_spec = pltpu.PrefetchScalarGridSpec(
        num_scalar_prefetch=2,
        grid=(batch * n_heads,),
        in_specs=[head, head, head, pl.BlockSpec(memory_space=pl.ANY), pl.BlockSpec(memory_space=pl.ANY)],
        out_specs=head,
        scratch_shapes=[pltpu.VMEM((2, n_rows, HEAD_DIM), F32), pltpu.VMEM((2, n_rows, HEAD_DIM), F32),
                        pltpu.SemaphoreType.DMA((2, 2))],
    )
    return pl.pallas_call(
        functools.partial(_page_attend_body, layer=layer, n_heads=n_heads, n_pages=n_pages, dec_seq=dec_seq),
        grid_spec=grid_spec,
        out_shape=jax.ShapeDtypeStruct((batch, dec_seq, d_att), F32),
        compiler_params=_params(1),
        name="page_attend",
    )(sel, page_table.reshape(-1), q, k_new, v_new, cache_k, cache_v)


def _shift_rows(x, d, fill):
    if d % SUBLANES == 0:
        return jnp.concatenate([jnp.full((d, x.shape[1]), fill, x.dtype), x[:x.shape[0] - d]], axis=0)
    row = lax.broadcasted_iota(jnp.int32, x.shape, 0)
    return jnp.where(row >= d, pltpu.roll(x, d, axis=0), fill)


def _lru_body(xl_ref, gl_ref, h0_ref, c0_ref, cw_ref, cb_ref, wa_ref, ba_ref, wx_ref, bx_ref, lam_ref, gol_ref,
              y_ref, hlast_ref, cnew_ref, xp_ref, h_ref, *, tc, n_blocks):
    step = pl.program_id(1)
    pad = SUBLANES
    hist = CONV_W - 1

    @pl.when(step == 0)
    def _():
        xp_ref[...] = jnp.zeros(xp_ref.shape, F32)
        xp_ref[pad - hist:pad, :] = c0_ref[0]
        h_ref[...] = h0_ref[0]

    xp_ref[pad:pad + tc, :] = xl_ref[...]
    xc = cb_ref[0]
    for j in range(CONV_W):
        xc = xc + xp_ref[pad - hist + j:pad - hist + j + tc, :] * cw_ref[0, j:j + 1, :]
    new_hist = xp_ref[pad + tc - hist:pad + tc, :]
    xp_ref[pad - hist:pad, :] = new_hist

    xcb = xc.astype(BF)
    ra, ix = [], []
    for n in range(n_blocks):
        sl = slice(n * LRU_BLOCK, (n + 1) * LRU_BLOCK)
        ra.append(_dot(xcb[:, sl], wa_ref[0, n].astype(BF)))
        ix.append(_dot(xcb[:, sl], wx_ref[0, n].astype(BF)))
    r = jax.nn.sigmoid(jnp.concatenate(ra, axis=-1) + ba_ref[0])
    i = jax.nn.sigmoid(jnp.concatenate(ix, axis=-1) + bx_ref[0])
    neg_lam = -lam_ref[0]
    softplus = jnp.maximum(neg_lam, 0.0) + jnp.log1p(jnp.exp(-jnp.abs(neg_lam)))
    log_a = (-LRU_C) * r * softplus
    a = jnp.exp(log_a)
    u = jnp.sqrt(-jnp.tanh(log_a) * (a * a + 1.0)) * (i * xc)

    d = 1
    while d < tc:
        a_prev = _shift_rows(a, d, 1.0)
        u_prev = _shift_rows(u, d, 0.0)
        u = a * u_prev + u
        a = a * a_prev
        d *= 2
    h = a * h_ref[...] + u
    h_ref[...] = h[tc - 1:tc, :]

    gl = gl_ref[...]
    gelu = 0.5 * gl * (1.0 + jnp.tanh(0.7978845608028654 * (gl + 0.044715 * (gl * gl * gl))))
    yl = h * gelu
    ms = jnp.mean(yl * yl, axis=-1, keepdims=True)
    y_ref[...] = yl * lax.rsqrt(ms + EPS) * gol_ref[0]

    @pl.when(step == pl.num_programs(1) - 1)
    def _():
        hlast_ref[0] = h[tc - 1:tc, :]
        cnew_ref[0] = new_hist


def _lru_mixer(proj, h0, conv0, conv_w, conv_b, w_a, b_a, w_x, b_x, lam, g_ol, layer, *, batch, seq, d_att, d_lru):
    assert d_att == d_lru
    m = proj.shape[0]
    tc = _tile(seq, T_LRU)
    per_b = seq // tc
    n_blocks = d_lru // LRU_BLOCK
    depth = conv_b.shape[0]
    vec = lambda a: a.reshape(depth, 1, d_lru)
    vec_spec = pl.BlockSpec((1, 1, d_lru), lambda b, s: (layer, 0, 0))
    gate_w = pl.BlockSpec((1, n_blocks, LRU_BLOCK, LRU_BLOCK), lambda b, s: (layer, 0, 0, 0))
    state = pl.BlockSpec((1, 1, d_lru), lambda b, s: (b, 0, 0))
    hist = pl.BlockSpec((1, CONV_W - 1, d_lru), lambda b, s: (b, 0, 0))
    return pl.pallas_call(
        functools.partial(_lru_body, tc=tc, n_blocks=n_blocks),
        grid=(batch, per_b),
        in_specs=[pl.BlockSpec((tc, d_lru), lambda b, s: (b * per_b + s, 3)),
                  pl.BlockSpec((tc, d_lru), lambda b, s: (b * per_b + s, 4)),
                  state, hist,
                  pl.BlockSpec((1, CONV_W, d_lru), lambda b, s: (layer, 0, 0)), vec_spec,
                  gate_w, vec_spec, gate_w, vec_spec, vec_spec, vec_spec],
        out_specs=[pl.BlockSpec((tc, d_lru), lambda b, s: (b * per_b + s, 0)), state, hist],
        out_shape=[jax.ShapeDtypeStruct((m, d_lru), F32),
                   jax.ShapeDtypeStruct((batch, 1, d_lru), F32),
                   jax.ShapeDtypeStruct((batch, CONV_W - 1, d_lru), F32)],
        scratch_shapes=[pltpu.VMEM((tc + SUBLANES, d_lru), F32), pltpu.VMEM((1, d_lru), F32)],
        compiler_params=_params(2),
        name="lru_mixer",
    )(proj, proj, h0.reshape(batch, 1, d_lru), conv0, conv_w, vec(conv_b), w_a, vec(b_a), w_x, vec(b_x),
      vec(lam), vec(g_ol))


def _out_proj_body(o_ref, yl_ref, goa_ref, w_ref, x_ref, gt_ref, xo_ref, wbf_ref, *, d_att):
    @pl.when(pl.program_id(1) == 0)
    def _():
        wbf_ref[...] = w_ref[0].astype(BF)

    o = o_ref[...]
    ms = jnp.mean(o * o, axis=-1, keepdims=True)
    ya = (o * lax.rsqrt(ms + EPS) * goa_ref[0]).astype(BF)
    y = _dot(ya, wbf_ref[0:d_att, :]) + _dot(yl_ref[...].astype(BF), wbf_ref[d_att:, :])
    xo_ref[...] = x_ref[...] + gt_ref[0] * y


def _out_proj(o_att, y_lru, g_oa, w_out, x, mod_g, layer, *, gate_chunk):
    m, d = x.shape
    d_att, d_lru = o_att.shape[1], y_lru.shape[1]
    groups, r, _ = mod_g.shape
    rows_per_group = m // groups
    tm = _tile(rows_per_group, TM_MATMUL // 2)
    assert r in (1, tm)
    per_group = rows_per_group // tm
    tn = _tile(d, TN_MATMUL)
    n_col = d // tn
    return pl.pallas_call(
        functools.partial(_out_proj_body, d_att=d_att),
        grid=(n_col, m // tm),
        in_specs=[pl.BlockSpec((tm, d_att), lambda j, i: (i, 0)),
                  pl.BlockSpec((tm, d_lru), lambda j, i: (i, 0)),
                  pl.BlockSpec((1, 1, d_att), lambda j, i: (layer, 0, 0)),
                  pl.BlockSpec((1, d, tn), lambda j, i: (layer, 0, j)),
                  pl.BlockSpec((tm, tn), lambda j, i: (i, j)),
                  pl.BlockSpec((1, r, tn), lambda j, i: (i // per_group, 0, gate_chunk * n_col + j))],
        out_specs=pl.BlockSpec((tm, tn), lambda j, i: (i, j)),
        out_shape=jax.ShapeDtypeStruct((m, d), F32),
        scratch_shapes=[pltpu.VMEM((d, tn), BF)],
        compiler_params=_params(2),
        name="out_proj",
    )(o_att, y_lru, g_oa.reshape(-1, 1, d_att), w_out, x, mod_g)


def _swiglu_partial(x, w1_ref, w3_ref, w2_ref):
    h1 = _dot(x, w1_ref.astype(BF))
    h3 = _dot(x, w3_ref.astype(BF))
    hidden = (h1 * jax.nn.sigmoid(h1) * h3).astype(BF)
    return _dot(hidden, w2_ref.astype(BF))


def _ffn_body(x_ref, w1_ref, w3_ref, w2_ref, o_ref):
    @pl.when(pl.program_id(1) == 0)
    def _():
        o_ref[...] = jnp.zeros(o_ref.shape, F32)

    o_ref[...] += _swiglu_partial(x_ref[...], w1_ref[0], w3_ref[0], w2_ref[0])


def _dense_ffn(xn, w1, w3, w2, idx):
    m, d = xn.shape
    d_ff = w1.shape[2]
    tm, tf = _tile(m, TM_FFN), _tile(d_ff, TF_FFN)
    return pl.pallas_call(
        _ffn_body,
        grid=(m // tm, d_ff // tf),
        in_specs=[pl.BlockSpec((tm, d), lambda i, f: (i, 0)),
                  pl.BlockSpec((1, d, tf), lambda i, f: (idx, 0, f)),
                  pl.BlockSpec((1, d, tf), lambda i, f: (idx, 0, f)),
                  pl.BlockSpec((1, tf, d), lambda i, f: (idx, f, 0))],
        out_specs=pl.BlockSpec((tm, d), lambda i, f: (i, 0)),
        out_shape=jax.ShapeDtypeStruct((m, d), F32),
        compiler_params=_params(2),
        name="dense_ffn",
    )(xn, w1, w3, w2)


def _router_body(x_ref, w_ref, idx_ref, p_ref, cnt_out_ref, cnt_ref, *, n_experts):
    @pl.when(pl.program_id(0) == 0)
    def _():
        cnt_ref[...] = jnp.zeros(cnt_ref.shape, F32)

    logits = _dot(x_ref[...].astype(BF), w_ref[0].astype(BF))
    lane = lax.broadcasted_iota(jnp.int32, logits.shape, 1)
    logits = jnp.where(lane < n_experts, logits, NEG_INF)
    top = []
    g = logits
    lane_f = lane.astype(F32)
    for _ in range(TOP_K):
        mx = jnp.max(g, axis=-1, keepdims=True)
        idx = jnp.min(jnp.where(g == mx, lane_f, float(LANES)), axis=-1, keepdims=True)
        g = jnp.where(lane_f == idx, NEG_INF, g)
        top.append((mx, idx))
    (v0, i0), (v1, i1) = top
    e1 = jnp.exp(v1 - v0)
    p0 = 1.0 / (1.0 + e1)
    p1 = e1 / (1.0 + e1)

    tm = logits.shape[0]
    tri = lax.broadcasted_iota(jnp.int32, (tm, tm), 0) >= lax.broadcasted_iota(jnp.int32, (tm, tm), 1)
    tri = jnp.where(tri, 1.0, 0.0).astype(BF)
    ranks, count = [], cnt_ref[...]
    for ik in (i0, i1):
        hot = jnp.where(lane_f == ik, 1.0, 0.0)
        prefix = _dot(tri, hot.astype(BF))
        ranks.append(jnp.sum((count + prefix - 1.0) * hot, axis=-1, keepdims=True))
        count = count + jnp.sum(hot, axis=0, keepdims=True)
    cnt_ref[...] = count
    cnt_out_ref[...] = count.astype(jnp.int32)
    out = jnp.where(lane == 0, i0, jnp.where(lane == 1, i1, jnp.where(lane == 2, ranks[0],
                                                                       jnp.where(lane == 3, ranks[1], 0.0))))
    idx_ref[...] = out.astype(jnp.int32)
    p_ref[...] = jnp.where(lane == 0, p0, jnp.where(lane == 1, p1, 0.0))


def _router(xn, router_w_pad, idx, n_experts):
    m, d = xn.shape
    tm = max(t for t in range(SUBLANES, TM_NORM + 1, SUBLANES) if m % t == 0)
    row = pl.BlockSpec((tm, LANES), lambda i: (i, 0))
    return pl.pallas_call(
        functools.partial(_router_body, n_experts=n_experts),
        grid=(m // tm,),
        in_specs=[pl.BlockSpec((tm, d), lambda i: (i, 0)), pl.BlockSpec((1, d, LANES), lambda i: (idx, 0, 0))],
        out_specs=[row, row, pl.BlockSpec((1, LANES), lambda i: (0, 0))],
        out_shape=[jax.ShapeDtypeStruct((m, LANES), jnp.int32), jax.ShapeDtypeStruct((m, LANES), F32),
                   jax.ShapeDtypeStruct((1, LANES), jnp.int32)],
        scratch_shapes=[pltpu.VMEM((1, LANES), F32)],
        compiler_params=_params(1),
        name="router",
    )(xn, router_w_pad)


def _expert_body(te_ref, used_ref, valid_ref, tok_ref, x_hbm, w1_ref, w3_ref, w2_ref, o_ref, xs_ref, xbf_ref, sem,
                 *, tm):
    t, f = pl.program_id(0), pl.program_id(1)
    half = tm // 2
    active = t < used_ref[0]
    small = jnp.logical_and(active, valid_ref[t] <= half)
    big = jnp.logical_and(active, valid_ref[t] > half)

    def gather(n_rows):
        def issue(r, c):
            tok = tok_ref[t * tm + r]
            pltpu.make_async_copy(x_hbm.at[pl.ds(tok, 1), :], xs_ref.at[pl.ds(r, 1), :], sem.at[0]).start()
            return c

        lax.fori_loop(0, n_rows, issue, 0)
        pltpu.make_async_copy(x_hbm.at[pl.ds(0, n_rows), :], xs_ref.at[pl.ds(0, n_rows), :], sem.at[0]).wait()
        xbf_ref[0:n_rows, :] = xs_ref[0:n_rows, :].astype(BF)

    @pl.when(jnp.logical_and(big, f == 0))
    def _():
        gather(tm)

    @pl.when(jnp.logical_and(small, f == 0))
    def _():
        gather(half)

    @pl.when(f == 0)
    def _():
        o_ref[...] = jnp.zeros(o_ref.shape, F32)

    @pl.when(big)
    def _():
        o_ref[...] += _swiglu_partial(xbf_ref[...], w1_ref[0, 0], w3_ref[0, 0], w2_ref[0, 0])

    @pl.when(small)
    def _():
        o_ref[0:half, :] += _swiglu_partial(xbf_ref[0:half, :], w1_ref[0, 0], w3_ref[0, 0], w2_ref[0, 0])


def _expert_ffn(xn, w1, w3, w2, idx, tile_expert, n_used, tile_valid, slot_token, *, tm):
    m, d = xn.shape
    d_ff = w1.shape[3]
    n_tiles = tile_expert.shape[0]
    tf = _tile(d_ff, TF_FFN)
    n_f = d_ff // tf
    assert tm % (2 * SUBLANES) == 0

    def fcol(t, f, used):
        return jnp.where(t < used[0], f, n_f - 1)

    grid_spec = pltpu.PrefetchScalarGridSpec(
        num_scalar_prefetch=4,
        grid=(n_tiles, n_f),
        in_specs=[pl.BlockSpec(memory_space=pl.ANY),
                  pl.BlockSpec((1, 1, d, tf), lambda t, f, te, used, nv, tok: (idx, te[t], 0, fcol(t, f, used))),
                  pl.BlockSpec((1, 1, d, tf), lambda t, f, te, used, nv, tok: (idx, te[t], 0, fcol(t, f, used))),
                  pl.BlockSpec((1, 1, tf, d), lambda t, f, te, used, nv, tok: (idx, te[t], fcol(t, f, used), 0))],
        out_specs=pl.BlockSpec((tm, d), lambda t, f, te, used, nv, tok: (t, 0)),
        scratch_shapes=[pltpu.VMEM((tm, d), F32), pltpu.VMEM((tm, d), BF), pltpu.SemaphoreType.DMA((1,))],
    )
    return pl.pallas_call(
        functools.partial(_expert_body, tm=tm),
        grid_spec=grid_spec,
        out_shape=jax.ShapeDtypeStruct((n_tiles * tm, d), F32),
        compiler_params=_params(2),
        name="expert_ffn",
    )(tile_expert, n_used, tile_valid, slot_token, xn, w1, w3, w2)


def _combine_body(pos_ref, ys_hbm, p_ref, o_ref, a_ref, b_ref, sem, *, tm):
    base = pl.program_id(0) * tm

    def issue(r, c):
        s0 = pos_ref[(base + r) * TOP_K]
        s1 = pos_ref[(base + r) * TOP_K + 1]
        pltpu.make_async_copy(ys_hbm.at[pl.ds(s0, 1), :], a_ref.at[pl.ds(r, 1), :], sem.at[0]).start()
        pltpu.make_async_copy(ys_hbm.at[pl.ds(s1, 1), :], b_ref.at[pl.ds(r, 1), :], sem.at[1]).start()
        return c

    lax.fori_loop(0, tm, issue, 0)
    pltpu.make_async_copy(ys_hbm.at[pl.ds(0, tm), :], a_ref, sem.at[0]).wait()
    pltpu.make_async_copy(ys_hbm.at[pl.ds(0, tm), :], b_ref, sem.at[1]).wait()
    o_ref[...] = p_ref[:, 0:1] * a_ref[...] + p_ref[:, 1:2] * b_ref[...]


def _combine(ys, pos, top_p, row_offset, m):
    d = ys.shape[1]
    tm = _tile(m, TM_COMBINE)
    assert row_offset % tm == 0
    first = row_offset // tm
    grid_spec = pltpu.PrefetchScalarGridSpec(
        num_scalar_prefetch=1,
        grid=(m // tm,),
        in_specs=[pl.BlockSpec(memory_space=pl.ANY), pl.BlockSpec((tm, LANES), lambda i, pos_: (first + i, 0))],
        out_specs=pl.BlockSpec((tm, d), lambda i, pos_: (i, 0)),
        scratch_shapes=[pltpu.VMEM((tm, d), F32), pltpu.VMEM((tm, d), F32), pltpu.SemaphoreType.DMA((2,))],
    )
    return pl.pallas_call(
        functools.partial(_combine_body, tm=tm),
        grid_spec=grid_spec,
        out_shape=jax.ShapeDtypeStruct((m, d), F32),
        compiler_params=_params(1),
        name="moe_combine",
    )(lax.slice(pos, (row_offset * TOP_K,), ((row_offset + m) * TOP_K,)), ys, top_p)


def _moe_ffn(xns, router_w_pad, w1, w3, w2, idx, n_experts):
    xn = jnp.concatenate(xns, axis=0) if len(xns) > 1 else xns[0]
    m = xn.shape[0]
    top_idx, top_p, counts = _router(xn, router_w_pad, idx, n_experts)
    e_flat = top_idx[:, :TOP_K].reshape(-1)
    rank = top_idx[:, TOP_K:2 * TOP_K].reshape(-1)
    counts = counts[0, :n_experts]
    n_assign = m * TOP_K
    tm = min(TM_FFN, m)
    n_tiles = -(-n_assign // tm) + n_experts
    experts = jnp.arange(n_experts, dtype=jnp.int32)
    tiles_e = (counts + tm - 1) // tm
    tile_end = jnp.cumsum(tiles_e)
    tile_start = tile_end - tiles_e
    one_hot = (e_flat[:, None] == experts[None, :]).astype(jnp.int32)
    slot = (jnp.sum(tile_start[None, :] * one_hot, axis=1) * tm + rank).astype(jnp.int32)
    slot_token = jnp.zeros((n_tiles * tm,), jnp.int32).at[slot].set(jnp.arange(n_assign, dtype=jnp.int32) // TOP_K)
    n_used = tile_end[-1]
    t_ids = jnp.minimum(jnp.arange(n_tiles, dtype=jnp.int32), n_used - 1)
    tile_expert = jnp.sum((t_ids[:, None] >= tile_end[None, :]).astype(jnp.int32), axis=1).astype(jnp.int32)
    own = (tile_expert[:, None] == experts[None, :]).astype(jnp.int32)
    left = counts[None, :] - (t_ids[:, None] - tile_start[None, :]) * tm
    tile_valid = jnp.clip(jnp.sum(own * left, axis=1), 0, tm).astype(jnp.int32)
    ys = _expert_ffn(xn, w1, w3, w2, idx, tile_expert, n_used.reshape(1).astype(jnp.int32), tile_valid,
                     slot_token, tm=tm)
    outs, start = [], 0
    for g in xns:
        rows = g.shape[0]
        outs.append(_combine(ys, slot, top_p, start, rows))
        start += rows
    return outs


def _rope_tables(pos):
    half = HEAD_DIM // 2
    inv = ROPE_THETA ** (-(jnp.arange(half, dtype=F32) / half))
    ang = pos.astype(F32)[:, None] * inv[None, :]
    cos, sin = jnp.cos(ang), jnp.sin(ang)
    return jnp.concatenate([cos, cos], axis=-1), jnp.concatenate([-sin, sin], axis=-1)


class _Trunk:
    def __init__(self, x3, mod_rows, pos, h_init, conv_init, paged, depth):
        self.batch, self.seq, self.d = x3.shape
        self.x = x3.reshape(self.batch * self.seq, self.d)
        self.cos, self.sin_signed = _rope_tables(pos)
        self.h_init, self.conv_init, self.paged = h_init, conv_init, paged
        if self.seq % TM_NORM == 0:
            self.mods = [mod_rows[l][:, None, :] for l in range(depth)]
        else:
            self.mods = [jnp.repeat(mod_rows[l], self.seq, axis=0)[None] for l in range(depth)]
        self.f = None
        self.k_rows, self.v_rows, self.h_fin, self.conv_fin = [], [], [], []

    def mix(self, l, w, moe_layer):
        batch, seq, mods = self.batch, self.seq, self.mods
        m = batch * seq
        d_att = w["g_out_att"].shape[1]
        d_lru = w["g_out_lru"].shape[1]
        n_heads = d_att // HEAD_DIM
        cos, sin_signed, paged = self.cos, self.sin_signed, self.paged
        if l == 0:
            x = self.x
            _, xn = _resid_norm(x, g_norm=w["g_norm_mix"], layer=l, norm_mod=mods[l], scale_chunk=1, shift_chunk=0)
        else:
            x, xn = _resid_norm(self.x, f=self.f, gate_mod=mods[l - 1], gate_chunk=5, g_norm=w["g_norm_mix"],
                                layer=l, norm_mod=mods[l], scale_chunk=1, shift_chunk=0)
        proj = _in_proj(xn, w["w_in"], l)
        if paged is None:
            k_new, v_new, q_t, k_hm, v_t, kmean = _qkv_prep(
                proj, w["g_q"], w["g_k"], cos, sin_signed, l, batch=batch, seq=seq, d_att=d_att, head_major=True)
            n_blocks = seq // MOBA_BLOCK
            km = kmean.reshape(batch, n_blocks, n_heads, HEAD_DIM).transpose(0, 2, 1, 3)
            km = jnp.pad(km, ((0, 0), (0, 0), (0, (-n_blocks) % SUBLANES), (0, 0)))
            o_att = _moba_attention(q_t, k_hm, v_t, km)
        else:
            cache_k, cache_v, page_table = paged
            k_new, v_new, q = _qkv_prep(
                proj, w["g_q"], w["g_k"], cos, sin_signed, l, batch=batch, seq=seq, d_att=d_att, head_major=False)
            q3 = q.reshape(batch, seq, d_att)
            sel = _page_select(q3, cache_k, page_table, l, n_heads=n_heads)
            sel_flat = sel[..., :MOBA_TOPK].reshape(-1)
            o_att = _page_attend(sel_flat, page_table, q3, k_new.reshape(batch, seq, d_att),
                                 v_new.reshape(batch, seq, d_att), cache_k, cache_v, l,
                                 n_heads=n_heads).reshape(m, d_att)
        y_lru, h_last, conv_new = _lru_mixer(
            proj, self.h_init[l], self.conv_init[l], w["conv_w"], w["conv_b"], w["w_gate_a"], w["b_gate_a"],
            w["w_gate_x"], w["b_gate_x"], w["lru_lambda"], w["g_out_lru"], l,
            batch=batch, seq=seq, d_att=d_att, d_lru=d_lru)
        self.x = _out_proj(o_att, y_lru, w["g_out_att"], w["w_out"], x, mods[l], l, gate_chunk=2)
        self.k_rows.append(k_new)
        self.v_rows.append(v_new)
        self.h_fin.append(h_last)
        self.conv_fin.append(conv_new)
        _, xn2 = _resid_norm(self.x, g_norm=w["g_norm_ffn"], layer=l, norm_mod=mods[l], scale_chunk=4,
                             shift_chunk=3, xn_dtype=F32 if moe_layer else BF)
        return xn2

    def finish(self, w):
        depth = len(self.k_rows)
        d_att = w["g_out_att"].shape[1]
        d_lru = w["g_out_lru"].shape[1]
        x, _ = _resid_norm(self.x, f=self.f, gate_mod=self.mods[depth - 1], gate_chunk=5)
        shape_kv = (depth, self.batch, self.seq, d_att // HEAD_DIM, HEAD_DIM)
        return (x.reshape(self.batch, self.seq, self.d), jnp.stack(self.k_rows).reshape(shape_kv),
                jnp.stack(self.v_rows).reshape(shape_kv),
                jnp.stack(self.h_fin).reshape(depth, self.batch, d_lru), jnp.stack(self.conv_fin))


def kernel(x_prompt, x_sample, cache_k, cache_v, state_lru_h, state_lru_conv, page_table, c_prompt, c_sample,
           g_norm_mix, g_norm_ffn, w_mod, b_mod, w_in, g_q, g_k, conv_w, conv_b, w_gate_a, b_gate_a, w_gate_x,
           b_gate_x, lru_lambda, g_out_att, g_out_lru, w_out, ffn_w1, ffn_w3, ffn_w2, router_w, moe_w1, moe_w3,
           moe_w2):
    depth, d = g_norm_mix.shape
    w = dict(g_norm_mix=g_norm_mix.reshape(depth, 1, d), g_norm_ffn=g_norm_ffn.reshape(depth, 1, d), w_in=w_in,
             g_q=g_q, g_k=g_k, conv_w=conv_w, conv_b=conv_b, w_gate_a=w_gate_a, b_gate_a=b_gate_a,
             w_gate_x=w_gate_x, b_gate_x=b_gate_x, lru_lambda=lru_lambda, g_out_att=g_out_att,
             g_out_lru=g_out_lru, w_out=w_out, ffn_w1=ffn_w1, ffn_w3=ffn_w3, ffn_w2=ffn_w2, router_w=router_w,
             moe_w1=moe_w1, moe_w3=moe_w3, moe_w2=moe_w2)
    b_p, s_p, _ = x_prompt.shape
    b_s, s_s, _ = x_sample.shape
    d_att = g_out_att.shape[1]
    d_lru = g_out_lru.shape[1]

    n_cond = b_p + b_s
    pad_rows = (-n_cond) % (2 * SUBLANES)
    c_all = jnp.concatenate([c_prompt, c_sample, jnp.zeros((pad_rows, d), F32)], axis=0)
    mod_all = _modulation(c_all, w_mod, b_mod)

    past_len = page_table.shape[1] * PAGE_SIZE
    pos_s = past_len + jnp.arange(s_s, dtype=jnp.int32)
    sample = _Trunk(x_sample, mod_all[:, b_p:n_cond], pos_s, state_lru_h, state_lru_conv,
                    (cache_k, cache_v, page_table), depth)
    pos_p = jnp.arange(s_p, dtype=jnp.int32)
    prompt = _Trunk(x_prompt, mod_all[:, :b_p], pos_p, jnp.zeros((depth, b_p, d_lru), F32),
                    jnp.zeros((depth, b_p, CONV_W - 1, d_lru), F32), None, depth)
    trunks = (sample, prompt)

    n_experts = router_w.shape[2]
    router_w_pad = jnp.pad(router_w, ((0, 0), (0, 0), (0, LANES - n_experts)))
    for l in range(depth):
        moe_layer = l % 2 == 1
        xn2 = [t.mix(l, w, moe_layer) for t in trunks]
        if moe_layer:
            fs = _moe_ffn(xn2[::-1], router_w_pad, moe_w1, moe_w3, moe_w2, l // 2, n_experts)[::-1]
        else:
            fs = [_dense_ffn(xn, ffn_w1, ffn_w3, ffn_w2, l // 2) for xn in xn2]
        for t, f in zip(trunks, fs):
            t.f = f
    y_s, k_s, v_s, h_s, conv_s = sample.finish(w)
    y_p, k_p, v_p, h_p, conv_p = prompt.finish(w)
    return (y_p, y_s, k_p, v_p, h_p, conv_p, k_s, v_s, h_s, conv_s)
```
